```python
import jax, jax.numpy as jnp
from jax import lax
import numpy as np

D_MODEL = 1024
BATCH = 16
SEQ = 2048
DEPTH = 2

MIX_WIDTH = D_MODEL
GROUP_WIDTH = MIX_WIDTH // 4
A_HEADS = 4
A_HEAD_DIM = GROUP_WIDTH // A_HEADS
A_CHUNK = 64
B_CONV_WIDTH = 31
C_CONV_WIDTH = 3
D_WINDOWS = (2, 4, 8, 16)
D_GROUPS = len(D_WINDOWS)
D_GROUP_DIM = GROUP_WIDTH // D_GROUPS
D_FF = ((8 * D_MODEL // 3 + 127) // 128) * 128
FFN_CONV_WIDTH = 3
A_COLS = 4 * GROUP_WIDTH
B_COLS = 2 * GROUP_WIDTH
C_COLS = 3 * GROUP_WIDTH
D_COLS = GROUP_WIDTH
IN_COLS = A_COLS + B_COLS + C_COLS + D_COLS
EPS = 1e-6
MIN_FORGET = 1e-30

kernel_name = "hybrid_parallel_hgrn2_conformer_shortconv_pool"


def rms_norm(x, g):
    xf = x.astype(jnp.float32)
    y = xf * lax.rsqrt(jnp.mean(xf * xf, axis=-1, keepdims=True) + EPS)
    return (y * g.astype(jnp.float32)).astype(x.dtype)


def layer_norm(x, g, b):
    xf = x.astype(jnp.float32)
    mu = jnp.mean(xf, axis=-1, keepdims=True)
    var = jnp.mean(jnp.square(xf - mu), axis=-1, keepdims=True)
    y = (xf - mu) * lax.rsqrt(var + EPS) * g.astype(jnp.float32) + b.astype(jnp.float32)
    return y.astype(x.dtype)


def causal_dwconv(x, w):
    width, ch = w.shape
    return lax.conv_general_dilated(
        x, w[:, None, :].astype(x.dtype), window_strides=(1,), padding=[(width - 1, 0)],
        dimension_numbers=("NWC", "WIO", "NWC"), feature_group_count=ch)


def hgrn2_chunked(q, k, v, log_f):
    bsz, t_len, h, kd = q.shape
    vd = v.shape[-1]
    n = t_len // A_CHUNK

    def to_chunks(a):
        return a.reshape(bsz, n, A_CHUNK, h, a.shape[-1]).transpose(1, 0, 3, 2, 4)

    qc, kc, vc, gc = to_chunks(q), to_chunks(k), to_chunks(v), to_chunks(log_f)
    mask = jnp.tril(jnp.ones((A_CHUNK, A_CHUNK), dtype=bool))[:, :, None]

    def step(state, inp):
        qi, ki, vi, gi = inp
        b = jnp.cumsum(gi, axis=-2)
        diff = b[..., :, None, :] - b[..., None, :, :]
        decay = jnp.where(mask, jnp.exp(jnp.where(mask, diff, 0.0)), 0.0)
        scores = jnp.einsum("bhtk,bhsk,bhtsk->bhts", qi, ki, decay)
        o_intra = jnp.einsum("bhts,bhsv->bhtv", scores, vi)
        o_inter = jnp.einsum("bhtk,bhkv->bhtv", qi * jnp.exp(b), state)
        b_last = b[..., -1:, :]
        new_state = jnp.exp(b_last)[..., 0, :, None] * state + jnp.einsum(
            "bhsk,bhsv->bhkv", ki * jnp.exp(b_last - b), vi)
        return new_state, o_intra + o_inter

    s0 = jnp.zeros((bsz, h, kd, vd), jnp.float32)
    _, o = lax.scan(step, s0, (qc, kc, vc, gc))
    return o.transpose(1, 0, 3, 2, 4).reshape(bsz, t_len, h, vd)


def hgrn2_mixer(p, lb, norm_g):
    bsz, t_len, _ = p.shape
    q_, f_, i_, g_ = jnp.split(p, 4, axis=-1)
    z = f_.astype(jnp.float32)
    lbf = lb.astype(jnp.float32)
    f = lbf + (1.0 - lbf) * jax.nn.sigmoid(z)
    log_f = jnp.log(jnp.maximum(f, MIN_FORGET))
    k = (1.0 - lbf) * jax.nn.sigmoid(-z)
    q = jax.nn.silu(q_.astype(jnp.float32)) * (A_HEAD_DIM ** -0.5)
    v = i_.astype(jnp.float32)
    heads = lambda a: a.reshape(bsz, t_len, A_HEADS, A_HEAD_DIM)
    o = hgrn2_chunked(heads(q), heads(k), heads(v), heads(log_f))
    o = o * lax.rsqrt(jnp.mean(o * o, axis=-1, keepdims=True) + EPS)
    o = o * norm_g.astype(jnp.float32).reshape(A_HEADS, A_HEAD_DIM)
    o = o.reshape(bsz, t_len, GROUP_WIDTH) * jax.nn.silu(g_.astype(jnp.float32))
    return o.astype(p.dtype)


def conformer_conv_mixer(p, dw_w, dw_b, ln_g, ln_b, pw_w, pw_b):
    a, gate = jnp.split(p, 2, axis=-1)
    h = a * jax.nn.sigmoid(gate)
    h = causal_dwconv(h, dw_w) + dw_b
    h = jax.nn.silu(layer_norm(h, ln_g, ln_b))
    return h @ pw_w + pw_b


def short_conv_mixer(p, conv_w):
    bg, cg, h = jnp.split(p, 3, axis=-1)
    return bg * causal_dwconv(cg * h, conv_w)


def pooling_mixer(u, proj, scale):
    bsz, t_len, ch = u.shape
    uf = u.astype(jnp.float32)
    cs = jnp.concatenate([jnp.zeros((bsz, 1, ch), jnp.float32), jnp.cumsum(uf, axis=1)], axis=1)
    pos = jnp.arange(t_len, dtype=jnp.float32)
    outs = []
    for g, w in enumerate(D_WINDOWS):
        sl = slice(g * D_GROUP_DIM, (g + 1) * D_GROUP_DIM)
        csg = cs[:, :, sl]
        lower = jnp.pad(csg[:, : t_len + 1 - w], ((0, 0), (w - 1, 0), (0, 0)))
        mean = (csg[:, 1:] - lower) / jnp.minimum(pos + 1.0, float(w))[None, :, None]
        outs.append(mean - uf[:, :, sl])
    pooled = jnp.stack(outs, axis=2)
    y = jnp.einsum("btgc,gcd->btgd", pooled, proj.astype(jnp.float32)).reshape(bsz, t_len, ch)
    return (y * scale.astype(jnp.float32)).astype(u.dtype)


def setup_inputs(seed: int = 0) -> dict:
    key = jax.random.key(seed)
    ks = jax.random.split(key, 21)
    f32 = jnp.float32
    nrm = lambda k, shape, s: jax.random.normal(k, shape, f32) * s
    gain = lambda k, shape: 1.0 + 0.1 * jax.random.normal(k, shape, f32)
    return {
        "x": nrm(ks[0], (BATCH, SEQ, D_MODEL), 1.0),
        "w_in": nrm(ks[1], (DEPTH, D_MODEL, IN_COLS), D_MODEL ** -0.5),
        "lb_gamma": nrm(ks[2], (DEPTH, GROUP_WIDTH), 0.5),
        "a_norm_g": gain(ks[3], (DEPTH, GROUP_WIDTH)),
        "b_dw_w": nrm(ks[4], (DEPTH, B_CONV_WIDTH, GROUP_WIDTH), B_CONV_WIDTH ** -0.5),
        "b_dw_b": nrm(ks[5], (DEPTH, GROUP_WIDTH), 0.02),
        "b_ln_g": gain(ks[6], (DEPTH, GROUP_WIDTH)),
        "b_ln_b": nrm(ks[7], (DEPTH, GROUP_WIDTH), 0.02),
        "b_pw_w": nrm(ks[8], (DEPTH, GROUP_WIDTH, GROUP_WIDTH), GROUP_WIDTH ** -0.5),
        "b_pw_b": nrm(ks[9], (DEPTH, GROUP_WIDTH), 0.02),
        "c_conv_w": nrm(ks[10], (DEPTH, C_CONV_WIDTH, GROUP_WIDTH), C_CONV_WIDTH ** -0.5),
        "d_proj": nrm(ks[11], (DEPTH, D_GROUPS, D_GROUP_DIM, D_GROUP_DIM), D_GROUP_DIM ** -0.5),
        "d_scale": gain(ks[12], (DEPTH, GROUP_WIDTH)),
        "w_out": nrm(ks[13], (DEPTH, MIX_WIDTH, D_MODEL), MIX_WIDTH ** -0.5),
        "mix_pre_g": gain(ks[14], (DEPTH, D_MODEL)),
        "mix_post_g": gain(ks[15], (DEPTH, D_MODEL)),
        "ffn_pre_g": gain(ks[16], (DEPTH, D_MODEL)),
        "ffn_post_g": gain(ks[17], (DEPTH, D_MODEL)),
        "w_up": nrm(ks[18], (DEPTH, D_MODEL, 2 * D_FF), D_MODEL ** -0.5),
        "ffn_conv_w": nrm(ks[19], (DEPTH, FFN_CONV_WIDTH, 2 * D_FF), FFN_CONV_WIDTH ** -0.5),
        "w_down": nrm(ks[20], (DEPTH, D_FF, D_MODEL), D_FF ** -0.5),
    }


def reference(x, w_in, lb_gamma, a_norm_g, b_dw_w, b_dw_b, b_ln_g, b_ln_b, b_pw_w, b_pw_b,
              c_conv_w, d_proj, d_scale, w_out, mix_pre_g, mix_post_g, ffn_pre_g, ffn_post_g,
              w_up, ffn_conv_w, w_down):
    lb_soft = jax.nn.softmax(lb_gamma.astype(jnp.float32), axis=0)
    lower_bounds = jnp.cumsum(lb_soft, axis=0) - lb_soft[0:1]
    o_a, o_b, o_c = A_COLS, A_COLS + B_COLS, A_COLS + B_COLS + C_COLS
    for l in range(DEPTH):
        h = rms_norm(x, mix_pre_g[l])
        p = h @ w_in[l]
        y_a = hgrn2_mixer(p[..., :o_a], lower_bounds[l], a_norm_g[l])
        y_b = conformer_conv_mixer(p[..., o_a:o_b], b_dw_w[l], b_dw_b[l], b_ln_g[l], b_ln_b[l],
                                   b_pw_w[l], b_pw_b[l])
        y_c = short_conv_mixer(p[..., o_b:o_c], c_conv_w[l])
        y_d = pooling_mixer(p[..., o_c:], d_proj[l], d_scale[l])
        y = jnp.concatenate([y_a, y_b, y_c, y_d], axis=-1) @ w_out[l]
        x = x + rms_norm(y, mix_post_g[l])
        h = rms_norm(x, ffn_pre_g[l])
        u = causal_dwconv(h @ w_up[l], ffn_conv_w[l])
        gate, up = jnp.split(u, 2, axis=-1)
        y = (jax.nn.silu(gate) * up) @ w_down[l]
        x = x + rms_norm(y, ffn_post_g[l])
    return x
```

```python
import functools

import jax
import jax.numpy as jnp
import numpy as np
from jax import lax
from jax.experimental import pallas as pl
from jax.experimental.pallas import tpu as pltpu

F32 = jnp.float32
BF16 = jnp.bfloat16

D_MODEL = 1024
GROUP_WIDTH = 256
A_HEADS = 4
A_HEAD_DIM = 64
A_CHUNK = 64
A_LEVELS = (32, 16, 8)
A_BAND = 8
B_CONV_WIDTH = 31
C_CONV_WIDTH = 3
D_WINDOWS = (2, 4, 8, 16)
D_FF = 2816
FFN_CONV_WIDTH = 3
IN_COLS = 10 * GROUP_WIDTH
EPS = 1e-6
MIN_FORGET = 1e-30

B_HALO = 32
C_HALO = 8
D_HALO = 16
SCAN_HALO = 32

MIX_TILE = 256
FFN_TILE = 512
FFN_CHUNK = 256
N_FFN_CHUNKS = D_FF // FFN_CHUNK
VMEM_LIMIT_BYTES = 56 * 1024 * 1024


def _dot(a, b):
    return jnp.dot(a, b, preferred_element_type=F32)


def _dot_nt(a, b):
    return lax.dot_general(a, b, (((1,), (1,)), ((), ())), preferred_element_type=F32)


def _dot_tn(a, b):
    return lax.dot_general(a, b, (((0,), (0,)), ((), ())), preferred_element_type=F32)


def _rms_norm(x, g):
    return x * lax.rsqrt(jnp.mean(x * x, axis=-1, keepdims=True) + EPS) * g


def _sigmoid(x):
    return 1.0 / (1.0 + jnp.exp(-x))


def _silu(x):
    return x * _sigmoid(x)


def _mixer_kernel(layer, x_ref, w_in_ref, lbg_ref, ang_ref, dww_ref, dwb_ref, lng_ref, lnb_ref,
                  pww_ref, pwb_ref, ccw_ref, dproj_ref, dscale_ref, w_out_ref, preg_ref, postg_ref,
                  bd16_ref, bd32_ref, lmask_ref, dwin_ref,
                  o_ref,
                  p_ref, ycat_ref, st_ref, scan_ref, kz_ref, vz_ref, fz_ref, pall_ref,
                  bbuf_ref, cbuf_ref, dbuf_ref):
    tt = x_ref.shape[1]
    ti = pl.program_id(1)
    gw = GROUP_WIDTH

    @pl.when(ti == 0)
    def _reset():
        st_ref[...] = jnp.zeros_like(st_ref)
        scan_ref[0:SCAN_HALO, :] = jnp.zeros((SCAN_HALO, gw), F32)
        kz_ref[0:A_BAND, :] = jnp.zeros((A_BAND, gw), F32)
        vz_ref[0:A_BAND, :] = jnp.zeros((A_BAND, gw), F32)
        fz_ref[0:A_BAND, :] = jnp.zeros((A_BAND, gw), F32)
        bbuf_ref[0:B_HALO, :] = jnp.zeros((B_HALO, gw), F32)
        cbuf_ref[0:C_HALO, :] = jnp.zeros((C_HALO, gw), F32)
        dbuf_ref[0:D_HALO, :] = jnp.zeros((D_HALO, gw), F32)

    x = x_ref[0]
    h = _rms_norm(x, preg_ref[...])
    p_ref[...] = _dot(h.astype(BF16), w_in_ref[...])

    lbg = lbg_ref[...]
    lbe = jnp.exp(lbg - jnp.max(lbg, axis=0, keepdims=True))
    lbs = lbe / jnp.sum(lbe, axis=0, keepdims=True)
    lb = jnp.sum(lbs[0:layer + 1, :], axis=0, keepdims=True) - lbs[0:1, :]
    one_m_lb = 1.0 - lb
    a_norm_g = ang_ref[...]
    bd16 = bd16_ref[...]
    bd32 = bd32_ref[...]

    def bcast_rows(row_ids):
        return jnp.concatenate(
            [jnp.broadcast_to(scan_ref[i:i + 1, :], (8, gw)) for i in row_ids], axis=0)

    def chunk_body(c, carry):
        r0 = pl.multiple_of(c * A_CHUNK, A_CHUNK)
        rows = pl.ds(r0, A_CHUNK)
        q_in = p_ref[rows, 0:gw]
        z = p_ref[rows, gw:2 * gw]
        v = p_ref[rows, 2 * gw:3 * gw]
        og = p_ref[rows, 3 * gw:4 * gw]

        f = lb + one_m_lb * _sigmoid(z)
        ft = jnp.maximum(f, MIN_FORGET)
        g = jnp.log(ft)
        k = one_m_lb * _sigmoid(-z)
        q = _silu(q_in) * (A_HEAD_DIM ** -0.5)

        lo, hi = SCAN_HALO, SCAN_HALO + A_CHUNK
        scan_ref[lo:hi, :] = g
        for j in range(6):
            sh = 1 << j
            scan_ref[lo:hi, :] = scan_ref[lo:hi, :] + scan_ref[lo - sh:hi - sh, :]
        b = scan_ref[lo:hi, :]
        b_last = scan_ref[hi - 1:hi, :]

        st = st_ref[...]
        o = _dot_nt((q * jnp.exp(b)).astype(BF16), st.astype(BF16))

        scores = jnp.zeros((A_CHUNK, gw), F32)
        for li, cs in enumerate(A_LEVELS):
            r_rows = [lo + (8 * r // cs) * cs - 1 for r in range(8)]
            e_rows = [lo + (8 * r // cs) * cs + cs - 1 for r in range(8)]
            qh = (q * jnp.exp(b - bcast_rows(r_rows))).astype(BF16)
            kh = (k * jnp.exp(bcast_rows(e_rows) - b)).astype(BF16)
            kbd = jnp.tile(kh, (A_HEADS, 1)) * bd16
            scores = scores + _dot_nt(qh, kbd) * lmask_ref[li]
        vbd = jnp.tile(v.astype(BF16), (A_HEADS, 1)) * bd16
        o = o + _dot(scores.astype(BF16), vbd)

        zl, zh = A_BAND, A_BAND + A_CHUNK
        kz_ref[zl:zh, :] = k
        vz_ref[zl:zh, :] = v
        fz_ref[zl:zh, :] = ft
        fprod = None
        for d in range(A_BAND):
            pd = q * kz_ref[zl - d:zh - d, :]
            if d >= 1:
                fs = fz_ref[zl - (d - 1):zh - (d - 1), :]
                fprod = fs if fprod is None else fprod * fs
                pd = pd * fprod
            pall_ref[d * A_CHUNK:(d + 1) * A_CHUNK, :] = pd.astype(BF16)
        band = _dot(pall_ref[...], bd16)
        for d in range(A_BAND):
            o = o + band[d * A_CHUNK:(d + 1) * A_CHUNK, :] * vz_ref[zl - d:zh - d, :]

        khat = (k * jnp.exp(b_last - b)).astype(BF16)
        upd = _dot_tn(v.astype(BF16), khat)
        st_ref[...] = st * jnp.exp(b_last) + upd * bd32

        o2 = o * o
        o2_hi = o2.astype(BF16)
        o2_lo = (o2 - o2_hi.astype(F32)).astype(BF16)
        ms = (_dot(o2_hi, bd16) + _dot(o2_lo, bd16)) * (1.0 / A_HEAD_DIM)
        ya = o * lax.rsqrt(ms + EPS) * a_norm_g * _silu(og)
        ycat_ref[rows, 0:gw] = ya.astype(BF16)
        return carry

    lax.fori_loop(0, tt // A_CHUNK, chunk_body, 0)

    o_b = 4 * gw
    hb = p_ref[:, o_b:o_b + gw] * _sigmoid(p_ref[:, o_b + gw:o_b + 2 * gw])
    bbuf_ref[B_HALO:B_HALO + tt, :] = hb
    blk = 64
    for rb in range(tt // blk):
        base = B_HALO + rb * blk - (B_CONV_WIDTH - 1)
        acc = jnp.zeros((blk, gw), F32)
        for j in range(B_CONV_WIDTH):
            acc = acc + dww_ref[j:j + 1, :] * bbuf_ref[base + j:base + j + blk, :]
        acc = acc + dwb_ref[...]
        mu = jnp.mean(acc, axis=-1, keepdims=True)
        cen = acc - mu
        var = jnp.mean(cen * cen, axis=-1, keepdims=True)
        yn = cen * lax.rsqrt(var + EPS) * lng_ref[...] + lnb_ref[...]
        yb = _dot(_silu(yn).astype(BF16), pww_ref[...]) + pwb_ref[...]
        ycat_ref[rb * blk:(rb + 1) * blk, gw:2 * gw] = yb.astype(BF16)
    bbuf_ref[0:B_HALO, :] = bbuf_ref[tt:tt + B_HALO, :]

    o_c = 6 * gw
    cbuf_ref[C_HALO:C_HALO + tt, :] = p_ref[:, o_c + gw:o_c + 2 * gw] * p_ref[:, o_c + 2 * gw:o_c + 3 * gw]
    conv_c = jnp.zeros((tt, gw), F32)
    for j in range(C_CONV_WIDTH):
        off = C_HALO - (C_CONV_WIDTH - 1) + j
        conv_c = conv_c + ccw_ref[j:j + 1, :] * cbuf_ref[off:off + tt, :]
    ycat_ref[:, 2 * gw:3 * gw] = (p_ref[:, o_c:o_c + gw] * conv_c).astype(BF16)
    cbuf_ref[0:C_HALO, :] = cbuf_ref[tt:tt + C_HALO, :]

    o_d = 9 * gw
    u = p_ref[:, o_d:o_d + gw]
    dbuf_ref[D_HALO:D_HALO + tt, :] = u
    half = gw // 2

    def window_sum(lane0, w_small, w_big):
        s = dbuf_ref[D_HALO:D_HALO + tt, lane0:lane0 + half]
        for j in range(1, w_small):
            s = s + dbuf_ref[D_HALO - j:D_HALO - j + tt, lane0:lane0 + half]
        s_big = s
        for j in range(w_small, w_big):
            s_big = s_big + dbuf_ref[D_HALO - j:D_HALO - j + tt, lane0:lane0 + half]
        lane = lax.broadcasted_iota(jnp.int32, (tt, half), 1)
        return jnp.where(lane < half // 2, s, s_big)

    wsum = jnp.concatenate([window_sum(0, D_WINDOWS[0], D_WINDOWS[1]),
                            window_sum(half, D_WINDOWS[2], D_WINDOWS[3])], axis=1)
    pos1 = (ti * tt + 1 + lax.broadcasted_iota(jnp.int32, (tt, gw), 0)).astype(F32)
    pooled = wsum / jnp.minimum(pos1, dwin_ref[...]) - u
    yd = _dot(pooled.astype(BF16), dproj_ref[...]) * dscale_ref[...]
    ycat_ref[:, 3 * gw:4 * gw] = yd.astype(BF16)
    dbuf_ref[0:D_HALO, :] = dbuf_ref[tt:tt + D_HALO, :]

    y = _dot(ycat_ref[...], w_out_ref[...])
    o_ref[0] = x + _rms_norm(y, postg_ref[...])


def _const_spec(shape):
    nd = len(shape)
    return pl.BlockSpec(shape, lambda b, t, _nd=nd: (0,) * _nd)


def _mixer_call(layer, x, w_in, lb_gamma, a_norm_g, dw_w, dw_b, ln_g, ln_b, pw_w, pw_b, cc_w,
                dproj_bd, d_scale, w_out, pre_g, post_g, bd16, bd32, lmask, dwin):
    bsz, t_len, d = x.shape
    tt = MIX_TILE
    assert t_len % tt == 0 and tt % A_CHUNK == 0
    gw = GROUP_WIDTH
    consts = (w_in, lb_gamma, a_norm_g, dw_w, dw_b, ln_g, ln_b, pw_w, pw_b, cc_w, dproj_bd,
              d_scale, w_out, pre_g, post_g, bd16, bd32, lmask, dwin)
    tile_spec = pl.BlockSpec((1, tt, d), lambda b, t: (b, t, 0))
    return pl.pallas_call(
        functools.partial(_mixer_kernel, layer),
        name=f"mixer_l{layer}",
        grid=(bsz, t_len // tt),
        in_specs=[tile_spec] + [_const_spec(c.shape) for c in consts],
        out_specs=tile_spec,
        out_shape=jax.ShapeDtypeStruct(x.shape, x.dtype),
        scratch_shapes=[
            pltpu.VMEM((tt, IN_COLS), F32),
            pltpu.VMEM((tt, d), BF16),
            pltpu.VMEM((gw, gw), F32),
            pltpu.VMEM((SCAN_HALO + A_CHUNK, gw), F32),
            pltpu.VMEM((A_BAND + A_CHUNK, gw), F32),
            pltpu.VMEM((A_BAND + A_CHUNK, gw), F32),
            pltpu.VMEM((A_BAND + A_CHUNK, gw), F32),
            pltpu.VMEM((A_BAND * A_CHUNK, gw), BF16),
            pltpu.VMEM((B_HALO + tt + B_HALO, gw), F32),
            pltpu.VMEM((C_HALO + tt + C_HALO, gw), F32),
            pltpu.VMEM((D_HALO + tt + D_HALO, gw), F32),
        ],
        compiler_params=pltpu.CompilerParams(
            dimension_semantics=("arbitrary", "arbitrary"),
            vmem_limit_bytes=VMEM_LIMIT_BYTES),
    )(x, *consts)


def _ffn_kernel(x_ref, w_up_ref, cw_ref, w_down_ref, preg_ref, postg_ref, o_ref,
                h_ref, ubuf_ref, uhdr_ref, act_ref):
    tt = x_ref.shape[1]
    ti = pl.program_id(1)
    cw2 = 2 * FFN_CHUNK

    @pl.when(ti == 0)
    def _reset():
        uhdr_ref[...] = jnp.zeros_like(uhdr_ref)

    x = x_ref[0]
    h_ref[...] = _rms_norm(x, preg_ref[...]).astype(BF16)

    def chunk_body(c, carry):
        ubuf_ref[0:C_HALO, :] = uhdr_ref[c]
        ubuf_ref[C_HALO:C_HALO + tt, :] = _dot(h_ref[...], w_up_ref[c])
        uhdr_ref[c] = ubuf_ref[tt:tt + C_HALO, :]
        cw = cw_ref[c]
        conv = jnp.zeros((tt, cw2), F32)
        for j in range(FFN_CONV_WIDTH):
            off = C_HALO - (FFN_CONV_WIDTH - 1) + j
            conv = conv + cw[j:j + 1, :] * ubuf_ref[off:off + tt, :]
        act = _silu(conv[:, 0:FFN_CHUNK]) * conv[:, FFN_CHUNK:cw2]
        act_ref[c] = act.astype(BF16)
        return carry

    lax.fori_loop(0, N_FFN_CHUNKS, chunk_body, 0)

    y = jnp.zeros((tt, D_MODEL), F32)
    for c in range(N_FFN_CHUNKS):
        y = y + _dot(act_ref[c], w_down_ref[c])
    o_ref[0] = x + _rms_norm(y, postg_ref[...])


def _ffn_call(x, w_up_c, cw_c, w_down_c, pre_g, post_g):
    bsz, t_len, d = x.shape
    tt = FFN_TILE
    assert t_len % tt == 0
    consts = (w_up_c, cw_c, w_down_c, pre_g, post_g)
    tile_spec = pl.BlockSpec((1, tt, d), lambda b, t: (b, t, 0))
    return pl.pallas_call(
        _ffn_kernel,
        name="ffn",
        grid=(bsz, t_len // tt),
        in_specs=[tile_spec] + [_const_spec(c.shape) for c in consts],
        out_specs=tile_spec,
        out_shape=jax.ShapeDtypeStruct(x.shape, x.dtype),
        scratch_shapes=[
            pltpu.VMEM((tt, d), BF16),
            pltpu.VMEM((C_HALO + tt, 2 * FFN_CHUNK), F32),
            pltpu.VMEM((N_FFN_CHUNKS, C_HALO, 2 * FFN_CHUNK), F32),
            pltpu.VMEM((N_FFN_CHUNKS, tt, FFN_CHUNK), BF16),
        ],
        compiler_params=pltpu.CompilerParams(
            dimension_semantics=("arbitrary", "arbitrary"),
            vmem_limit_bytes=VMEM_LIMIT_BYTES),
    )(x, *consts)


def _mixer_constants():
    hs = np.arange(GROUP_WIDTH)
    bd = (hs[:, None] // A_HEAD_DIM == hs[None, :] // A_HEAD_DIM).astype(np.float32)
    t = np.arange(A_CHUNK)[:, None]
    s = (np.arange(GROUP_WIDTH) % A_CHUNK)[None, :]
    lmask = np.stack([((t // c) % 2 == 1) & (s // c == t // c - 1) & (t - s >= A_BAND)
                      for c in A_LEVELS]).astype(np.float32)
    dwin = np.repeat(np.asarray(D_WINDOWS, np.float32), GROUP_WIDTH // len(D_WINDOWS))[None, :]
    return jnp.asarray(bd, BF16), jnp.asarray(bd), jnp.asarray(lmask), jnp.asarray(dwin)


def _block_diag(proj):
    g, n, _ = proj.shape
    out = jnp.zeros((g * n, g * n), proj.dtype)
    for i in range(g):
        out = out.at[i * n:(i + 1) * n, i * n:(i + 1) * n].set(proj[i])
    return out


def _chunk_gate_up(w):
    lead = w.shape[:-1]
    w = w.reshape(lead + (2, N_FFN_CHUNKS, FFN_CHUNK))
    w = jnp.moveaxis(w, -2, 0)
    return w.reshape((N_FFN_CHUNKS,) + lead + (2 * FFN_CHUNK,))


def kernel(x, w_in, lb_gamma, a_norm_g, b_dw_w, b_dw_b, b_ln_g, b_ln_b, b_pw_w, b_pw_b, c_conv_w, d_proj, d_scale, w_out, mix_pre_g, mix_post_g, ffn_pre_g, ffn_post_g, w_up, ffn_conv_w, w_down):
    depth = w_in.shape[0]
    bd16, bd32, lmask, dwin = _mixer_constants()
    row = lambda a: a[None, :]
    for l in range(depth):
        x = _mixer_call(
            l, x, w_in[l].astype(BF16), lb_gamma, row(a_norm_g[l]), b_dw_w[l], row(b_dw_b[l]),
            row(b_ln_g[l]), row(b_ln_b[l]), b_pw_w[l].astype(BF16), row(b_pw_b[l]), c_conv_w[l],
            _block_diag(d_proj[l]).astype(BF16), row(d_scale[l]), w_out[l].astype(BF16),
            row(mix_pre_g[l]), row(mix_post_g[l]), bd16, bd32, lmask, dwin)
        x = _ffn_call(
            x, _chunk_gate_up(w_up[l].astype(BF16)), _chunk_gate_up(ffn_conv_w[l]),
            w_down[l].astype(BF16).reshape(N_FFN_CHUNKS, FFN_CHUNK, D_MODEL),
            row(ffn_pre_g[l]), row(ffn_post_g[l]))
    return x
```

```python
import functools

import jax
import jax.numpy as jnp
import numpy as np
from jax import lax
from jax.experimental import pallas as pl
from jax.experimental.pallas import tpu as pltpu

F32 = jnp.float32
BF16 = jnp.bfloat16

LANES = 128
D_MODEL = 1024
GROUP_WIDTH = 256
GROUP_SLABS = GROUP_WIDTH // LANES
A_HEADS = 4
A_HEAD_DIM = 64
A_CHUNK = 64
A_LEVELS = (32, 16, 8)
A_BAND = 8
B_CONV_WIDTH = 31
C_CONV_WIDTH = 3
D_WINDOWS = (2, 4, 8, 16)
D_FF = 2816
FFN_CONV_WIDTH = 3
IN_COLS = 10 * GROUP_WIDTH
EPS = 1e-6
MIN_FORGET = 1e-30

B_HALO = 32
C_HALO = 8
D_HALO = 16
SCAN_HALO = 32

MIX_TILE = 256
MIX_BLOCK = 64
FFN_TILE = 512
FFN_CHUNK = 256
FFN_SLABS = 2 * FFN_CHUNK // LANES
N_FFN_CHUNKS = D_FF // FFN_CHUNK
VMEM_LIMIT_BYTES = 56 * 1024 * 1024


def _dot(a, b):
    return jnp.dot(a, b, preferred_element_type=F32)


def _dot_nt(a, b):
    return lax.dot_general(a, b, (((1,), (1,)), ((), ())), preferred_element_type=F32)


def _dot_tn(a, b):
    return lax.dot_general(a, b, (((0,), (0,)), ((), ())), preferred_element_type=F32)


def _rms_norm(x, g):
    return x * lax.rsqrt(jnp.mean(x * x, axis=-1, keepdims=True) + EPS) * g


def _sigmoid(x):
    return 1.0 / (1.0 + jnp.exp(-x))


def _silu(x):
    return x * _sigmoid(x)


def _put(ref, lead, r0, val):
    n = val.shape[0]
    for l in range(val.shape[1] // LANES):
        ref[lead + (l, slice(r0, r0 + n), slice(None))] = val[:, l * LANES:(l + 1) * LANES]


def _get(ref, lead, r0, n):
    slabs = ref.shape[len(lead)]
    return jnp.concatenate(
        [ref[lead + (l, slice(r0, r0 + n), slice(None))] for l in range(slabs)], axis=1)


def _mixer_kernel(layer, x_ref, w_in_ref, lbg_ref, ang_ref, dww_ref, dwb_ref, lng_ref, lnb_ref,
                  pww_ref, pwb_ref, ccw_ref, dproj_ref, dscale_ref, w_out_ref, preg_ref, postg_ref,
                  bd16_ref, bd32_ref, lmask_ref, dwin_ref,
                  o_ref,
                  p_ref, ycat_ref, st_ref, scan_ref, kz_ref, vz_ref, fz_ref, pall_ref,
                  bbuf_ref, cbuf_ref, dbuf_ref):
    tt = x_ref.shape[1]
    ti = pl.program_id(1)
    gw = GROUP_WIDTH

    @pl.when(ti == 0)
    def _reset():
        st_ref[...] = jnp.zeros_like(st_ref)
        scan_ref[:, 0:SCAN_HALO, :] = jnp.zeros((GROUP_SLABS, SCAN_HALO, LANES), F32)
        kz_ref[:, 0:A_BAND, :] = jnp.zeros((GROUP_SLABS, A_BAND, LANES), F32)
        vz_ref[:, 0:A_BAND, :] = jnp.zeros((GROUP_SLABS, A_BAND, LANES), F32)
        fz_ref[:, 0:A_BAND, :] = jnp.zeros((GROUP_SLABS, A_BAND, LANES), F32)
        bbuf_ref[:, 0:B_HALO, :] = jnp.zeros((GROUP_SLABS, B_HALO, LANES), F32)
        cbuf_ref[:, 0:C_HALO, :] = jnp.zeros((GROUP_SLABS, C_HALO, LANES), F32)
        dbuf_ref[:, 0:D_HALO, :] = jnp.zeros((GROUP_SLABS, D_HALO, LANES), F32)

    x = x_ref[0]
    h = _rms_norm(x, preg_ref[...]).astype(BF16)
    for lo_c, hi_c in ((0, 4 * gw), (4 * gw, 6 * gw), (6 * gw, 9 * gw), (9 * gw, 10 * gw)):
        p_ref[:, lo_c:hi_c] = _dot(h, w_in_ref[:, lo_c:hi_c])

    lbg = lbg_ref[...]
    lbe = jnp.exp(lbg - jnp.max(lbg, axis=0, keepdims=True))
    lbs = lbe / jnp.sum(lbe, axis=0, keepdims=True)
    lb = jnp.sum(lbs[0:layer + 1, :], axis=0, keepdims=True) - lbs[0:1, :]
    one_m_lb = 1.0 - lb
    a_norm_g = ang_ref[...]
    bd16 = bd16_ref[...]
    bd32 = bd32_ref[...]

    def bcast_rows(row_ids):
        return jnp.concatenate(
            [jnp.broadcast_to(_get(scan_ref, (), i, 1), (8, gw)) for i in row_ids], axis=0)

    def hgrn_chunk(c):
        rows = slice(c * A_CHUNK, (c + 1) * A_CHUNK)
        q_in = p_ref[rows, 0:gw]
        z = p_ref[rows, gw:2 * gw]
        v = p_ref[rows, 2 * gw:3 * gw]
        og = p_ref[rows, 3 * gw:4 * gw]

        f = lb + one_m_lb * _sigmoid(z)
        ft = jnp.maximum(f, MIN_FORGET)
        g = jnp.log(ft)
        k = one_m_lb * _sigmoid(-z)
        q = _silu(q_in) * (A_HEAD_DIM ** -0.5)

        lo = SCAN_HALO
        _put(scan_ref, (), lo, g)
        for j in range(6):
            sh = 1 << j
            _put(scan_ref, (), lo, _get(scan_ref, (), lo, A_CHUNK) + _get(scan_ref, (), lo - sh, A_CHUNK))
        b = _get(scan_ref, (), lo, A_CHUNK)
        b_last = _get(scan_ref, (), lo + A_CHUNK - 1, 1)

        st = st_ref[...]
        o = _dot_nt((q * jnp.exp(b)).astype(BF16), st.astype(BF16))

        scores = jnp.zeros((A_CHUNK, gw), F32)
        for li, cs in enumerate(A_LEVELS):
            r_rows = [lo + (8 * r // cs) * cs - 1 for r in range(8)]
            e_rows = [lo + (8 * r // cs) * cs + cs - 1 for r in range(8)]
            qh = (q * jnp.exp(b - bcast_rows(r_rows))).astype(BF16)
            kh = (k * jnp.exp(bcast_rows(e_rows) - b)).astype(BF16)
            kbd = jnp.tile(kh, (A_HEADS, 1)) * bd16
            scores = scores + _dot_nt(qh, kbd) * lmask_ref[li]
        vbd = jnp.tile(v.astype(BF16), (A_HEADS, 1)) * bd16
        o = o + _dot(scores.astype(BF16), vbd)

        zl = A_BAND
        _put(kz_ref, (), zl, k)
        _put(vz_ref, (), zl, v)
        _put(fz_ref, (), zl, ft)
        fprod = None
        for d in range(A_BAND):
            pd = q * _get(kz_ref, (), zl - d, A_CHUNK)
            if d >= 1:
                fs = _get(fz_ref, (), zl - (d - 1), A_CHUNK)
                fprod = fs if fprod is None else fprod * fs
                pd = pd * fprod
            pall_ref[d * A_CHUNK:(d + 1) * A_CHUNK, :] = pd.astype(BF16)
        band = _dot(pall_ref[...], bd16)
        for d in range(A_BAND):
            o = o + band[d * A_CHUNK:(d + 1) * A_CHUNK, :] * _get(vz_ref, (), zl - d, A_CHUNK)

        khat = (k * jnp.exp(b_last - b)).astype(BF16)
        upd = _dot_tn(v.astype(BF16), khat)
        st_ref[...] = st * jnp.exp(b_last) + upd * bd32

        o2 = o * o
        o2_hi = o2.astype(BF16)
        o2_lo = (o2 - o2_hi.astype(F32)).astype(BF16)
        ms = (_dot(o2_hi, bd16) + _dot(o2_lo, bd16)) * (1.0 / A_HEAD_DIM)
        ya = o * lax.rsqrt(ms + EPS) * a_norm_g * _silu(og)
        ycat_ref[rows, 0:gw] = ya.astype(BF16)

    o_b = 4 * gw

    def conformer_block(rb):
        blk = MIX_BLOCK
        base = B_HALO + rb * blk - (B_CONV_WIDTH - 1)
        acc = jnp.zeros((blk, gw), F32)
        for j in range(B_CONV_WIDTH):
            acc = acc + dww_ref[j:j + 1, :] * _get(bbuf_ref, (), base + j, blk)
        acc = acc + dwb_ref[...]
        mu = jnp.mean(acc, axis=-1, keepdims=True)
        cen = acc - mu
        var = jnp.mean(cen * cen, axis=-1, keepdims=True)
        yn = cen * lax.rsqrt(var + EPS) * lng_ref[...] + lnb_ref[...]
        yb = _dot(_silu(yn).astype(BF16), pww_ref[...]) + pwb_ref[...]
        ycat_ref[rb * blk:(rb + 1) * blk, gw:2 * gw] = yb.astype(BF16)

    _put(bbuf_ref, (), B_HALO, p_ref[:, o_b:o_b + gw] * _sigmoid(p_ref[:, o_b + gw:o_b + 2 * gw]))
    for c in range(tt // A_CHUNK):
        hgrn_chunk(c)
        for rb in range(c * A_CHUNK // MIX_BLOCK, (c + 1) * A_CHUNK // MIX_BLOCK):
            conformer_block(rb)
    _put(bbuf_ref, (), 0, _get(bbuf_ref, (), tt, B_HALO))

    o_c = 6 * gw
    _put(cbuf_ref, (), C_HALO, p_ref[:, o_c + gw:o_c + 2 * gw] * p_ref[:, o_c + 2 * gw:o_c + 3 * gw])
    conv_c = jnp.zeros((tt, gw), F32)
    for j in range(C_CONV_WIDTH):
        off = C_HALO - (C_CONV_WIDTH - 1) + j
        conv_c = conv_c + ccw_ref[j:j + 1, :] * _get(cbuf_ref, (), off, tt)
    ycat_ref[:, 2 * gw:3 * gw] = (p_ref[:, o_c:o_c + gw] * conv_c).astype(BF16)
    _put(cbuf_ref, (), 0, _get(cbuf_ref, (), tt, C_HALO))

    o_d = 9 * gw
    u = p_ref[:, o_d:o_d + gw]
    _put(dbuf_ref, (), D_HALO, u)

    def window_sum(slab, w_small, w_big):
        s = dbuf_ref[slab, D_HALO:D_HALO + tt, :]
        for j in range(1, w_small):
            s = s + dbuf_ref[slab, D_HALO - j:D_HALO - j + tt, :]
        s_big = s
        for j in range(w_small, w_big):
            s_big = s_big + dbuf_ref[slab, D_HALO - j:D_HALO - j + tt, :]
        lane = lax.broadcasted_iota(jnp.int32, (tt, LANES), 1)
        return jnp.where(lane < LANES // 2, s, s_big)

    wsum = jnp.concatenate([window_sum(0, D_WINDOWS[0], D_WINDOWS[1]),
                            window_sum(1, D_WINDOWS[2], D_WINDOWS[3])], axis=1)
    pos1 = (ti * tt + 1 + lax.broadcasted_iota(jnp.int32, (tt, gw), 0)).astype(F32)
    pooled = wsum / jnp.minimum(pos1, dwin_ref[...]) - u
    yd = _dot(pooled.astype(BF16), dproj_ref[...]) * dscale_ref[...]
    ycat_ref[:, 3 * gw:4 * gw] = yd.astype(BF16)
    _put(dbuf_ref, (), 0, _get(dbuf_ref, (), tt, D_HALO))

    y = _dot(ycat_ref[...], w_out_ref[...])
    o_ref[0] = x + _rms_norm(y, postg_ref[...])


def _const_spec(shape):
    nd = len(shape)
    return pl.BlockSpec(shape, lambda b, t, _nd=nd: (0,) * _nd)


def _mixer_call(layer, x, w_in, lb_gamma, a_norm_g, dw_w, dw_b, ln_g, ln_b, pw_w, pw_b, cc_w,
                dproj_bd, d_scale, w_out, pre_g, post_g, bd16, bd32, lmask, dwin):
    bsz, t_len, d = x.shape
    tt = MIX_TILE
    assert t_len % tt == 0 and tt % A_CHUNK == 0 and A_CHUNK % MIX_BLOCK == 0
    gw = GROUP_WIDTH
    consts = (w_in, lb_gamma, a_norm_g, dw_w, dw_b, ln_g, ln_b, pw_w, pw_b, cc_w, dproj_bd,
              d_scale, w_out, pre_g, post_g, bd16, bd32, lmask, dwin)
    tile_spec = pl.BlockSpec((1, tt, d), lambda b, t: (b, t, 0))
    slab = lambda rows: pltpu.VMEM((GROUP_SLABS, rows, LANES), F32)
    return pl.pallas_call(
        functools.partial(_mixer_kernel, layer),
        name=f"mixer_l{layer}",
        grid=(bsz, t_len // tt),
        in_specs=[tile_spec] + [_const_spec(c.shape) for c in consts],
        out_specs=tile_spec,
        out_shape=jax.ShapeDtypeStruct(x.shape, x.dtype),
        scratch_shapes=[
            pltpu.VMEM((tt, IN_COLS), F32),
            pltpu.VMEM((tt, d), BF16),
            pltpu.VMEM((gw, gw), F32),
            slab(SCAN_HALO + A_CHUNK),
            slab(A_BAND + A_CHUNK),
            slab(A_BAND + A_CHUNK),
            slab(A_BAND + A_CHUNK),
            pltpu.VMEM((A_BAND * A_CHUNK, gw), BF16),
            slab(B_HALO + tt),
            slab(C_HALO + tt),
            slab(D_HALO + tt),
        ],
        compiler_params=pltpu.CompilerParams(
            dimension_semantics=("arbitrary", "arbitrary"),
            vmem_limit_bytes=VMEM_LIMIT_BYTES),
    )(x, *consts)


def _ffn_kernel(x_ref, w_up_ref, cw_ref, w_down_ref, preg_ref, postg_ref, o_ref,
                h_ref, ubuf_ref, uhdr_ref, act_ref):
    tt = x_ref.shape[1]
    ti = pl.program_id(1)

    @pl.when(ti == 0)
    def _reset():
        uhdr_ref[...] = jnp.zeros_like(uhdr_ref)

    x = x_ref[0]
    h_ref[...] = _rms_norm(x, preg_ref[...]).astype(BF16)

    def up_project(c):
        s = c % 2
        ubuf_ref[s, :, 0:C_HALO, :] = uhdr_ref[c]
        _put(ubuf_ref, (s,), C_HALO, _dot(h_ref[...], w_up_ref[c]))
        uhdr_ref[c] = ubuf_ref[s, :, tt:tt + C_HALO, :]

    def conv_act(c):
        s = c % 2
        cw = cw_ref[c]
        conv = jnp.zeros((tt, 2 * FFN_CHUNK), F32)
        for j in range(FFN_CONV_WIDTH):
            off = C_HALO - (FFN_CONV_WIDTH - 1) + j
            conv = conv + cw[j:j + 1, :] * _get(ubuf_ref, (s,), off, tt)
        act = _silu(conv[:, 0:FFN_CHUNK]) * conv[:, FFN_CHUNK:2 * FFN_CHUNK]
        act_ref[:, c * FFN_CHUNK:(c + 1) * FFN_CHUNK] = act.astype(BF16)

    up_project(0)
    for c in range(N_FFN_CHUNKS):
        if c + 1 < N_FFN_CHUNKS:
            up_project(c + 1)
        conv_act(c)

    y = _dot(act_ref[...], w_down_ref[...])
    o_ref[0] = x + _rms_norm(y, postg_ref[...])


def _ffn_call(x, w_up_c, cw_c, w_down, pre_g, post_g):
    bsz, t_len, d = x.shape
    tt = FFN_TILE
    assert t_len % tt == 0
    consts = (w_up_c, cw_c, w_down, pre_g, post_g)
    tile_spec = pl.BlockSpec((1, tt, d), lambda b, t: (b, t, 0))
    return pl.pallas_call(
        _ffn_kernel,
        name="ffn",
        grid=(bsz, t_len // tt),
        in_specs=[tile_spec] + [_const_spec(c.shape) for c in consts],
        out_specs=tile_spec,
        out_shape=jax.ShapeDtypeStruct(x.shape, x.dtype),
        scratch_shapes=[
            pltpu.VMEM((tt, d), BF16),
            pltpu.VMEM((2, FFN_SLABS, C_HALO + tt, LANES), F32),
            pltpu.VMEM((N_FFN_CHUNKS, FFN_SLABS, C_HALO, LANES), F32),
            pltpu.VMEM((tt, D_FF), BF16),
        ],
        compiler_params=pltpu.CompilerParams(
            dimension_semantics=("arbitrary", "arbitrary"),
            vmem_limit_bytes=VMEM_LIMIT_BYTES),
    )(x, *consts)


def _mixer_constants():
    hs = np.arange(GROUP_WIDTH)
    bd = (hs[:, None] // A_HEAD_DIM == hs[None, :] // A_HEAD_DIM).astype(np.float32)
    t = np.arange(A_CHUNK)[:, None]
    s = (np.arange(GROUP_WIDTH) % A_CHUNK)[None, :]
    lmask = np.stack([((t // c) % 2 == 1) & (s // c == t // c - 1) & (t - s >= A_BAND)
                      for c in A_LEVELS]).astype(np.float32)
    dwin = np.repeat(np.asarray(D_WINDOWS, np.float32), GROUP_WIDTH // len(D_WINDOWS))[None, :]
    return jnp.asarray(bd, BF16), jnp.asarray(bd), jnp.asarray(lmask), jnp.asarray(dwin)


def _block_diag(proj):
    g, n, _ = proj.shape
    out = jnp.zeros((g * n, g * n), proj.dtype)
    for i in range(g):
        out = out.at[i * n:(i + 1) * n, i * n:(i + 1) * n].set(proj[i])
    return out


def _chunk_gate_up(w):
    lead = w.shape[:-1]
    w = w.reshape(lead + (2, N_FFN_CHUNKS, FFN_CHUNK))
    w = jnp.moveaxis(w, -2, 0)
    return w.reshape((N_FFN_CHUNKS,) + lead + (2 * FFN_CHUNK,))


def kernel(x, w_in, lb_gamma, a_norm_g, b_dw_w, b_dw_b, b_ln_g, b_ln_b, b_pw_w, b_pw_b, c_conv_w, d_proj, d_scale, w_out, mix_pre_g, mix_post_g, ffn_pre_g, ffn_post_g, w_up, ffn_conv_w, w_down):
    depth = w_in.shape[0]
    bd16, bd32, lmask, dwin = _mixer_constants()
    row = lambda a: a[None, :]
    for l in range(depth):
        x = _mixer_call(
            l, x, w_in[l].astype(BF16), lb_gamma, row(a_norm_g[l]), b_dw_w[l], row(b_dw_b[l]),
            row(b_ln_g[l]), row(b_ln_b[l]), b_pw_w[l].astype(BF16), row(b_pw_b[l]), c_conv_w[l],
            _block_diag(d_proj[l]).astype(BF16), row(d_scale[l]), w_out[l].astype(BF16),
            row(mix_pre_g[l]), row(mix_post_g[l]), bd16, bd32, lmask, dwin)
        x = _ffn_call(
            x, _chunk_gate_up(w_up[l].astype(BF16)), _chunk_gate_up(ffn_conv_w[l]),
            w_down[l].astype(BF16), row(ffn_pre_g[l]), row(ffn_post_g[l]))
    return x
```

```python
import functools

import jax
import jax.numpy as jnp
import numpy as np
from jax import lax
from jax.experimental import pallas as pl
from jax.experimental.pallas import tpu as pltpu

F32 = jnp.float32
BF16 = jnp.bfloat16

LANES = 128
D_MODEL = 1024
GROUP_WIDTH = 256
GROUP_SLABS = GROUP_WIDTH // LANES
A_HEADS = 4
A_HEAD_DIM = 64
A_CHUNK = 64
A_LEVELS = (32, 16, 8)
A_BAND = 8
B_CONV_WIDTH = 31
C_CONV_WIDTH = 3
D_WINDOWS = (2, 4, 8, 16)
D_FF = 2816
FFN_CONV_WIDTH = 3
IN_COLS = 10 * GROUP_WIDTH
EPS = 1e-6
MIN_FORGET = 1e-30

B_HALO = 32
C_HALO = 8
D_HALO = 16
SCAN_HALO = 32

MIX_TILE = 256
MIX_BLOCK = 64
FFN_TILE = 512
FFN_CHUNK = 256
FFN_SLABS = 2 * FFN_CHUNK // LANES
N_FFN_CHUNKS = D_FF // FFN_CHUNK
VMEM_LIMIT_BYTES = 56 * 1024 * 1024


_dot = functools.partial(jnp.dot, preferred_element_type=F32)
_dot_nt = functools.partial(lax.dot_general, dimension_numbers=(((1,), (1,)), ((), ())),
                            preferred_element_type=F32)
_dot_tn = functools.partial(lax.dot_general, dimension_numbers=(((0,), (0,)), ((), ())),
                            preferred_element_type=F32)


def _rms_norm(x, g):
    return x * lax.rsqrt(jnp.mean(x * x, axis=-1, keepdims=True) + EPS) * g


def _sigmoid(x):
    return 1.0 / (1.0 + jnp.exp(-x))


def _silu(x):
    return x * _sigmoid(x)


def _put(ref, lead, r0, val, slab0=0):
    n = val.shape[0]
    for l in range(val.shape[1] // LANES):
        ref[lead + (slab0 + l, slice(r0, r0 + n), slice(None))] = val[:, l * LANES:(l + 1) * LANES]


def _get(ref, lead, r0, n, slab0=0, slabs=None):
    slabs = ref.shape[len(lead)] - slab0 if slabs is None else slabs
    return jnp.concatenate(
        [ref[lead + (slab0 + l, slice(r0, r0 + n), slice(None))] for l in range(slabs)], axis=1)


def _mixer_kernel(layer, x_ref, w_in_ref, lbg_ref, ang_ref, dww_ref, dwb_ref, lng_ref, lnb_ref,
                  pww_ref, pwb_ref, ccw_ref, dproj_ref, dscale_ref, w_out_ref, preg_ref, postg_ref,
                  bd16_ref, bd32_ref, lmask_ref, dwin_ref,
                  o_ref,
                  p_ref, ycat_ref, st_ref, scan_ref, kz_ref, vz_ref, fz_ref, pall_ref,
                  bbuf_ref, cbuf_ref, dbuf_ref):
    tt = x_ref.shape[1]
    ti = pl.program_id(1)
    gw = GROUP_WIDTH
    n_chunks = tt // A_CHUNK
    n_lev = len(A_LEVELS)

    @pl.when(ti == 0)
    def _reset():
        st_ref[...] = jnp.zeros_like(st_ref)
        scan_ref[:, :, 0:SCAN_HALO, :] = jnp.zeros((n_chunks, GROUP_SLABS, SCAN_HALO, LANES), F32)
        kz_ref[:, :, 0:A_BAND, :] = jnp.zeros((n_chunks, GROUP_SLABS, A_BAND, LANES), F32)
        vz_ref[:, :, 0:A_BAND, :] = jnp.zeros((n_chunks, GROUP_SLABS, A_BAND, LANES), F32)
        fz_ref[:, :, 0:A_BAND, :] = jnp.zeros((n_chunks, GROUP_SLABS, A_BAND, LANES), F32)
        bbuf_ref[:, 0:B_HALO, :] = jnp.zeros((GROUP_SLABS, B_HALO, LANES), F32)
        cbuf_ref[:, 0:C_HALO, :] = jnp.zeros((GROUP_SLABS, C_HALO, LANES), F32)
        dbuf_ref[:, 0:D_HALO, :] = jnp.zeros((GROUP_SLABS, D_HALO, LANES), F32)

    lbg = lbg_ref[...]
    lbe = jnp.exp(lbg - jnp.max(lbg, axis=0, keepdims=True))
    lbs = lbe / jnp.sum(lbe, axis=0, keepdims=True)
    lb = jnp.sum(lbs[0:layer + 1, :], axis=0, keepdims=True) - lbs[0:1, :]
    one_m_lb = 1.0 - lb
    a_norm_g = ang_ref[...]
    bd16 = bd16_ref[...]
    bd32 = bd32_ref[...]
    groups = ((0, 4 * gw), (4 * gw, 6 * gw), (6 * gw, 9 * gw), (9 * gw, 10 * gw))
    o_b, o_c, o_d = 4 * gw, 6 * gw, 9 * gw
    lo = SCAN_HALO
    zl = A_BAND

    def chunk_rows(c):
        return slice(c * A_CHUNK, (c + 1) * A_CHUNK)

    def block_diag(val):
        return jnp.tile(val, (A_HEADS, 1)) * bd16

    def bcast_rows(c, row_ids):
        return jnp.concatenate(
            [jnp.broadcast_to(_get(scan_ref, (c,), r, 1), (8, gw)) for r in row_ids], axis=0)

    def hgrn_prepare():
        st8 = {"q": [], "k": [], "v": [], "b2": [], "b2_last": []}
        for c in range(n_chunks):
            rows = chunk_rows(c)
            z = p_ref[rows, gw:2 * gw]
            f = lb + one_m_lb * _sigmoid(z)
            ft = jnp.maximum(f, MIN_FORGET)
            k = 1.0 - f
            q = _silu(p_ref[rows, 0:gw]) * (A_HEAD_DIM ** -0.5)
            v = p_ref[rows, 2 * gw:3 * gw]
            _put(scan_ref, (c,), lo, jnp.log2(ft))
            _put(kz_ref, (c,), zl, k)
            _put(vz_ref, (c,), zl, v)
            _put(fz_ref, (c,), zl, ft)
            st8["q"].append(q); st8["k"].append(k); st8["v"].append(v)
        for j in range(6):
            sh = 1 << j
            for c in range(n_chunks):
                _put(scan_ref, (c,), lo,
                     _get(scan_ref, (c,), lo, A_CHUNK) + _get(scan_ref, (c,), lo - sh, A_CHUNK))
        for c in range(n_chunks):
            st8["b2"].append(_get(scan_ref, (c,), lo, A_CHUNK))
            st8["b2_last"].append(_get(scan_ref, (c,), lo + A_CHUNK - 1, 1))
        return st8

    def hgrn_pair_dots(s8):
        s8["levels"], s8["band"], s8["upd"] = [], [], []
        for c in range(n_chunks):
            q, k, v, b2 = s8["q"][c], s8["k"][c], s8["v"][c], s8["b2"][c]
            per_level = []
            for li, cs in enumerate(A_LEVELS):
                r_rows = [lo + (8 * r // cs) * cs - 1 for r in range(8)]
                e_rows = [lo + (8 * r // cs) * cs + cs - 1 for r in range(8)]
                qh = (q * jnp.exp2(b2 - bcast_rows(c, r_rows))).astype(BF16)
                kh = (k * jnp.exp2(bcast_rows(c, e_rows) - b2)).astype(BF16)
                per_level.append(_dot_nt(qh, block_diag(kh)))
            s8["levels"].append(per_level)
        for c in range(n_chunks):
            q = s8["q"][c]
            fprod = None
            for d in range(A_BAND):
                pd = q * _get(kz_ref, (c,), zl - d, A_CHUNK)
                if d >= 1:
                    fs = _get(fz_ref, (c,), zl - (d - 1), A_CHUNK)
                    fprod = fs if fprod is None else fprod * fs
                    pd = pd * fprod
                pall_ref[c, d * A_CHUNK:(d + 1) * A_CHUNK, :] = pd.astype(BF16)
            s8["band"].append(_dot(pall_ref[c], bd16))
        for c in range(n_chunks):
            khat = (s8["k"][c] * jnp.exp2(s8["b2_last"][c] - s8["b2"][c])).astype(BF16)
            s8["upd"].append(_dot_tn(s8["v"][c].astype(BF16), khat))

    def hgrn_outputs(s8):
        sts = [st_ref[...]]
        for c in range(n_chunks):
            sts.append(sts[c] * jnp.exp2(s8["b2_last"][c]) + s8["upd"][c] * bd32)
        st_ref[...] = sts[n_chunks]
        s8["o"] = []
        for c in range(n_chunks):
            q, v = s8["q"][c], s8["v"][c]
            scores = sum(s8["levels"][c][li] * lmask_ref[li] for li in range(n_lev))
            o = _dot_nt((q * jnp.exp2(s8["b2"][c])).astype(BF16), sts[c].astype(BF16))
            o = o + _dot(scores.astype(BF16), block_diag(v.astype(BF16)))
            for d in range(A_BAND):
                o = o + s8["band"][c][d * A_CHUNK:(d + 1) * A_CHUNK, :] * _get(vz_ref, (c,), zl - d, A_CHUNK)
            s8["o"].append(o)

    def hgrn_finish(s8):
        sq = []
        for c in range(n_chunks):
            o2 = s8["o"][c] * s8["o"][c]
            o2_hi = o2.astype(BF16)
            o2_lo = (o2 - o2_hi.astype(F32)).astype(BF16)
            sq.append(_dot(o2_hi, bd16) + _dot(o2_lo, bd16))
        for c in range(n_chunks):
            rows = chunk_rows(c)
            ms = sq[c] * (1.0 / A_HEAD_DIM)
            ya = s8["o"][c] * lax.rsqrt(ms + EPS) * a_norm_g * _silu(p_ref[rows, 3 * gw:4 * gw])
            ycat_ref[rows, 0:gw] = ya.astype(BF16)

    def conformer():
        _put(bbuf_ref, (), B_HALO, p_ref[:, o_b:o_b + gw] * _sigmoid(p_ref[:, o_b + gw:o_b + 2 * gw]))
        blk = MIX_BLOCK
        for rb in range(tt // blk):
            r0 = rb * blk
            base = B_HALO + r0 - (B_CONV_WIDTH - 1)
            acc = jnp.zeros((blk, gw), F32)
            for j in range(B_CONV_WIDTH):
                acc = acc + dww_ref[j:j + 1, :] * _get(bbuf_ref, (), base + j, blk)
            acc = acc + dwb_ref[...]
            mu = jnp.mean(acc, axis=-1, keepdims=True)
            cen = acc - mu
            var = jnp.mean(cen * cen, axis=-1, keepdims=True)
            yn = cen * lax.rsqrt(var + EPS) * lng_ref[...] + lnb_ref[...]
            yb = _dot(_silu(yn).astype(BF16), pww_ref[...]) + pwb_ref[...]
            ycat_ref[r0:r0 + blk, gw:2 * gw] = yb.astype(BF16)

    def short_conv():
        _put(cbuf_ref, (), C_HALO, p_ref[:, o_c + gw:o_c + 2 * gw] * p_ref[:, o_c + 2 * gw:o_c + 3 * gw])
        conv_c = jnp.zeros((tt, gw), F32)
        for j in range(C_CONV_WIDTH):
            off = C_HALO - (C_CONV_WIDTH - 1) + j
            conv_c = conv_c + ccw_ref[j:j + 1, :] * _get(cbuf_ref, (), off, tt)
        ycat_ref[:, 2 * gw:3 * gw] = (p_ref[:, o_c:o_c + gw] * conv_c).astype(BF16)

    def pooling():
        n = tt
        u = p_ref[:, o_d:o_d + gw]
        base = D_HALO
        _put(dbuf_ref, (), base, u)

        def window_sum(slab, w_small, w_big):
            s = dbuf_ref[slab, base:base + n, :]
            for j in range(1, w_small):
                s = s + dbuf_ref[slab, base - j:base - j + n, :]
            s_big = s
            for j in range(w_small, w_big):
                s_big = s_big + dbuf_ref[slab, base - j:base - j + n, :]
            lane = lax.broadcasted_iota(jnp.int32, (n, LANES), 1)
            return jnp.where(lane < LANES // 2, s, s_big)

        wsum = jnp.concatenate([window_sum(0, D_WINDOWS[0], D_WINDOWS[1]),
                                window_sum(1, D_WINDOWS[2], D_WINDOWS[3])], axis=1)
        pos1 = (ti * tt + 1 + lax.broadcasted_iota(jnp.int32, (n, gw), 0)).astype(F32)
        pooled = wsum / jnp.minimum(pos1, dwin_ref[...]) - u
        yd = _dot(pooled.astype(BF16), dproj_ref[...]) * dscale_ref[...]
        ycat_ref[:, 3 * gw:4 * gw] = yd.astype(BF16)

    h = _rms_norm(x_ref[0], preg_ref[...]).astype(BF16)
    for lo_c, hi_c in groups:
        p_ref[:, lo_c:hi_c] = _dot(h, w_in_ref[:, lo_c:hi_c])
    s8 = hgrn_prepare()
    hgrn_pair_dots(s8)
    hgrn_outputs(s8)
    hgrn_finish(s8)
    conformer()
    short_conv()
    pooling()
    y = _dot(ycat_ref[...], w_out_ref[...])
    o_ref[0] = x_ref[0] + _rms_norm(y, postg_ref[...])

    _put(bbuf_ref, (), 0, _get(bbuf_ref, (), tt, B_HALO))
    _put(cbuf_ref, (), 0, _get(cbuf_ref, (), tt, C_HALO))
    _put(dbuf_ref, (), 0, _get(dbuf_ref, (), tt, D_HALO))


def _const_spec(shape):
    nd = len(shape)
    return pl.BlockSpec(shape, lambda b, t, _nd=nd: (0,) * _nd)


def _mixer_call(layer, x, w_in, lb_gamma, a_norm_g, dw_w, dw_b, ln_g, ln_b, pw_w, pw_b, cc_w,
                dproj_bd, d_scale, w_out, pre_g, post_g, bd16, bd32, lmask, dwin):
    bsz, t_len, d = x.shape
    tt = MIX_TILE
    assert t_len % tt == 0 and tt % A_CHUNK == 0 and A_CHUNK % MIX_BLOCK == 0
    gw = GROUP_WIDTH
    consts = (w_in, lb_gamma, a_norm_g, dw_w, dw_b, ln_g, ln_b, pw_w, pw_b, cc_w, dproj_bd,
              d_scale, w_out, pre_g, post_g, bd16, bd32, lmask, dwin)
    tile_spec = pl.BlockSpec((1, tt, d), lambda b, t: (b, t, 0))
    slab = lambda rows: pltpu.VMEM((GROUP_SLABS, rows, LANES), F32)
    n_chunks = tt // A_CHUNK
    chunk_slab = lambda rows: pltpu.VMEM((n_chunks, GROUP_SLABS, rows, LANES), F32)
    return pl.pallas_call(
        functools.partial(_mixer_kernel, layer),
        name=f"mixer_l{layer}",
        grid=(bsz, t_len // tt),
        in_specs=[tile_spec] + [_const_spec(c.shape) for c in consts],
        out_specs=tile_spec,
        out_shape=jax.ShapeDtypeStruct(x.shape, x.dtype),
        scratch_shapes=[
            pltpu.VMEM((tt, IN_COLS), F32),
            pltpu.VMEM((tt, d), BF16),
            pltpu.VMEM((gw, gw), F32),
            chunk_slab(SCAN_HALO + A_CHUNK),
            chunk_slab(A_BAND + A_CHUNK),
            chunk_slab(A_BAND + A_CHUNK),
            chunk_slab(A_BAND + A_CHUNK),
            pltpu.VMEM((n_chunks, A_BAND * A_CHUNK, gw), BF16),
            slab(B_HALO + tt),
            slab(C_HALO + tt),
            slab(D_HALO + tt),
        ],
        compiler_params=pltpu.CompilerParams(
            dimension_semantics=("arbitrary", "arbitrary"),
            vmem_limit_bytes=VMEM_LIMIT_BYTES),
    )(x, *consts)


def _ffn_kernel(x_ref, w_up_ref, cw_ref, w_down_ref, preg_ref, postg_ref, o_ref,
                h_ref, ubuf_ref, uhdr_ref, act_ref):
    tt = x_ref.shape[1]
    ti = pl.program_id(1)
    half_slabs = FFN_CHUNK // LANES
    halves = (slice(0, tt // 2), slice(tt // 2, tt))

    @pl.when(ti == 0)
    def _reset():
        uhdr_ref[...] = jnp.zeros_like(uhdr_ref)

    for rs in halves:
        h_ref[rs, :] = _rms_norm(x_ref[0, rs, :], preg_ref[...]).astype(BF16)

    def columns(c):
        return (c * FFN_CHUNK, D_FF + c * FFN_CHUNK)

    def up_project(c, row_slices):
        s = c % 2
        ubuf_ref[s, :, 0:C_HALO, :] = uhdr_ref[c]
        for rs in row_slices:
            for part, col0 in enumerate(columns(c)):
                u = _dot(h_ref[rs, :], w_up_ref[:, col0:col0 + FFN_CHUNK])
                _put(ubuf_ref, (s,), C_HALO + rs.start, u, slab0=part * half_slabs)
        uhdr_ref[c] = ubuf_ref[s, :, tt:tt + C_HALO, :]

    def conv_act(c):
        s = c % 2
        for rs in halves:
            n = rs.stop - rs.start
            convs = []
            for part, col0 in enumerate(columns(c)):
                conv = jnp.zeros((n, FFN_CHUNK), F32)
                for j in range(FFN_CONV_WIDTH):
                    off = C_HALO - (FFN_CONV_WIDTH - 1) + j + rs.start
                    conv = conv + cw_ref[j:j + 1, col0:col0 + FFN_CHUNK] * _get(
                        ubuf_ref, (s,), off, n, slab0=part * half_slabs, slabs=half_slabs)
                convs.append(conv)
            act = _silu(convs[0]) * convs[1]
            act_ref[rs, c * FFN_CHUNK:(c + 1) * FFN_CHUNK] = act.astype(BF16)

    up_project(0, halves)
    for c in range(N_FFN_CHUNKS):
        if c + 1 < N_FFN_CHUNKS:
            up_project(c + 1, (slice(0, tt),))
        conv_act(c)

    for rs in halves:
        y = _dot(act_ref[rs, :], w_down_ref[...])
        o_ref[0, rs, :] = x_ref[0, rs, :] + _rms_norm(y, postg_ref[...])


def _ffn_call(x, w_up, conv_w, w_down, pre_g, post_g):
    bsz, t_len, d = x.shape
    tt = FFN_TILE
    assert t_len % tt == 0
    consts = (w_up, conv_w, w_down, pre_g, post_g)
    tile_spec = pl.BlockSpec((1, tt, d), lambda b, t: (b, t, 0))
    return pl.pallas_call(
        _ffn_kernel,
        name="ffn",
        grid=(bsz, t_len // tt),
        in_specs=[tile_spec] + [_const_spec(c.shape) for c in consts],
        out_specs=tile_spec,
        out_shape=jax.ShapeDtypeStruct(x.shape, x.dtype),
        scratch_shapes=[
            pltpu.VMEM((tt, d), BF16),
            pltpu.VMEM((2, FFN_SLABS, C_HALO + tt, LANES), F32),
            pltpu.VMEM((N_FFN_CHUNKS, FFN_SLABS, C_HALO, LANES), F32),
            pltpu.VMEM((tt, D_FF), BF16),
        ],
        compiler_params=pltpu.CompilerParams(
            dimension_semantics=("arbitrary", "arbitrary"),
            vmem_limit_bytes=VMEM_LIMIT_BYTES),
    )(x, *consts)


def _mixer_constants():
    hs = np.arange(GROUP_WIDTH)
    bd = (hs[:, None] // A_HEAD_DIM == hs[None, :] // A_HEAD_DIM).astype(np.float32)
    t = np.arange(A_CHUNK)[:, None]
    s = (np.arange(GROUP_WIDTH) % A_CHUNK)[None, :]
    lmask = np.stack([((t // c) % 2 == 1) & (s // c == t // c - 1) & (t - s >= A_BAND)
                      for c in A_LEVELS]).astype(np.float32)
    dwin = np.repeat(np.asarray(D_WINDOWS, np.float32), GROUP_WIDTH // len(D_WINDOWS))[None, :]
    return jnp.asarray(bd, BF16), jnp.asarray(bd), jnp.asarray(lmask), jnp.asarray(dwin)


def _block_diag(proj):
    g, n, _ = proj.shape
    out = jnp.zeros((g * n, g * n), proj.dtype)
    for i in range(g):
        out = out.at[i * n:(i + 1) * n, i * n:(i + 1) * n].set(proj[i])
    return out


def kernel(x, w_in, lb_gamma, a_norm_g, b_dw_w, b_dw_b, b_ln_g, b_ln_b, b_pw_w, b_pw_b, c_conv_w, d_proj, d_scale, w_out, mix_pre_g, mix_post_g, ffn_pre_g, ffn_post_g, w_up, ffn_conv_w, w_down):
    depth = w_in.shape[0]
    bd16, bd32, lmask, dwin = _mixer_constants()
    row = lambda a: a[None, :]
    for l in range(depth):
        x = _mixer_call(
            l, x, w_in[l].astype(BF16), lb_gamma, row(a_norm_g[l]), b_dw_w[l], row(b_dw_b[l]),
            row(b_ln_g[l]), row(b_ln_b[l]), b_pw_w[l].astype(BF16), row(b_pw_b[l]), c_conv_w[l],
            _block_diag(d_proj[l]).astype(BF16), row(d_scale[l]), w_out[l].astype(BF16),
            row(mix_pre_g[l]), row(mix_post_g[l]), bd16, bd32, lmask, dwin)
        x = _ffn_call(
            x, w_up[l].astype(BF16), ffn_conv_w[l], w_down[l].astype(BF16),
            row(ffn_pre_g[l]), row(ffn_post_g[l]))
    return x
```

```python
import functools

import jax
import jax.numpy as jnp
import numpy as np
from jax import lax
from jax.experimental import pallas as pl
from jax.experimental.pallas import tpu as pltpu

F32 = jnp.float32
BF16 = jnp.bfloat16

LANES = 128
D_MODEL = 1024
GROUP_WIDTH = 256
GROUP_SLABS = GROUP_WIDTH // LANES
A_HEADS = 4
A_HEAD_DIM = 64
A_CHUNK = 64
A_LEVELS = (32, 16, 8)
A_BAND = 8
B_CONV_WIDTH = 31
C_CONV_WIDTH = 3
D_WINDOWS = (2, 4, 8, 16)
D_FF = 2816
FFN_CONV_WIDTH = 3
IN_COLS = 10 * GROUP_WIDTH
EPS = 1e-6
MIN_FORGET = 1e-30

B_HALO = 32
C_HALO = 8
D_HALO = 16
SCAN_HALO = 32

MIX_TILE = 512
MIX_BLOCK = 64
FFN_TILE = 512
FFN_CHUNK = 256
FFN_SLABS = 2 * FFN_CHUNK // LANES
N_FFN_CHUNKS = D_FF // FFN_CHUNK
VMEM_LIMIT_BYTES = 56 * 1024 * 1024


_dot = functools.partial(jnp.dot, preferred_element_type=F32)
_dot_nt = functools.partial(lax.dot_general, dimension_numbers=(((1,), (1,)), ((), ())),
                            preferred_element_type=F32)
_dot_tn = functools.partial(lax.dot_general, dimension_numbers=(((0,), (0,)), ((), ())),
                            preferred_element_type=F32)


def _rms_norm(x, g):
    return x * lax.rsqrt(jnp.mean(x * x, axis=-1, keepdims=True) + EPS) * g


def _sigmoid(x):
    return 1.0 / (1.0 + jnp.exp(-x))


def _silu(x):
    return x * _sigmoid(x)


def _put(ref, lead, r0, val, slab0=0):
    n = val.shape[0]
    for l in range(val.shape[1] // LANES):
        ref[lead + (slab0 + l, slice(r0, r0 + n), slice(None))] = val[:, l * LANES:(l + 1) * LANES]


def _get(ref, lead, r0, n, slab0=0, slabs=None):
    slabs = ref.shape[len(lead)] - slab0 if slabs is None else slabs
    return jnp.concatenate(
        [ref[lead + (slab0 + l, slice(r0, r0 + n), slice(None))] for l in range(slabs)], axis=1)


def _mixer_kernel(layer, x_ref, w_in_ref, lbg_ref, ang_ref, dww_ref, dwb_ref, lng_ref, lnb_ref,
                  pww_ref, pwb_ref, ccw_ref, dproj_ref, dscale_ref, w_out_ref, preg_ref, postg_ref,
                  bd16_ref, bd32_ref, lmask_ref, dwin_ref,
                  o_ref,
                  p_ref, ycat_ref, st_ref, scan_ref, kz_ref, vz_ref, fz_ref, pall_ref,
                  bbuf_ref, cbuf_ref, dbuf_ref):
    tt = x_ref.shape[1]
    ti = pl.program_id(1)
    gw = GROUP_WIDTH
    n_chunks = tt // A_CHUNK
    n_lev = len(A_LEVELS)

    @pl.when(ti == 0)
    def _reset():
        st_ref[...] = jnp.zeros_like(st_ref)
        scan_ref[:, :, 0:SCAN_HALO, :] = jnp.zeros((n_chunks, GROUP_SLABS, SCAN_HALO, LANES), F32)
        kz_ref[:, :, 0:A_BAND, :] = jnp.zeros((n_chunks, GROUP_SLABS, A_BAND, LANES), F32)
        vz_ref[:, :, 0:A_BAND, :] = jnp.zeros((n_chunks, GROUP_SLABS, A_BAND, LANES), F32)
        fz_ref[:, :, 0:A_BAND, :] = jnp.zeros((n_chunks, GROUP_SLABS, A_BAND, LANES), F32)
        bbuf_ref[:, 0:B_HALO, :] = jnp.zeros((GROUP_SLABS, B_HALO, LANES), F32)
        cbuf_ref[:, 0:C_HALO, :] = jnp.zeros((GROUP_SLABS, C_HALO, LANES), F32)
        dbuf_ref[:, 0:D_HALO, :] = jnp.zeros((GROUP_SLABS, D_HALO, LANES), F32)

    lbg = lbg_ref[...]
    lbe = jnp.exp(lbg - jnp.max(lbg, axis=0, keepdims=True))
    lbs = lbe / jnp.sum(lbe, axis=0, keepdims=True)
    lb = jnp.sum(lbs[0:layer + 1, :], axis=0, keepdims=True) - lbs[0:1, :]
    one_m_lb = 1.0 - lb
    a_norm_g = ang_ref[...]
    bd16 = bd16_ref[...]
    bd32 = bd32_ref[...]
    groups = ((0, 4 * gw), (4 * gw, 6 * gw), (6 * gw, 9 * gw), (9 * gw, 10 * gw))
    o_b, o_c, o_d = 4 * gw, 6 * gw, 9 * gw
    lo = SCAN_HALO
    zl = A_BAND

    def chunk_rows(c):
        return slice(c * A_CHUNK, (c + 1) * A_CHUNK)

    def block_diag(val):
        return jnp.tile(val, (A_HEADS, 1)) * bd16

    def bcast_rows(c, row_ids):
        return jnp.concatenate(
            [jnp.broadcast_to(_get(scan_ref, (c,), r, 1), (8, gw)) for r in row_ids], axis=0)

    def hgrn_prepare():
        st8 = {"q": [], "k": [], "v": [], "b2": [], "b2_last": []}
        for c in range(n_chunks):
            rows = chunk_rows(c)
            z = p_ref[rows, gw:2 * gw]
            f = lb + one_m_lb * _sigmoid(z)
            ft = jnp.maximum(f, MIN_FORGET)
            k = 1.0 - f
            q = _silu(p_ref[rows, 0:gw]) * (A_HEAD_DIM ** -0.5)
            v = p_ref[rows, 2 * gw:3 * gw]
            _put(scan_ref, (c,), lo, jnp.log2(ft))
            _put(kz_ref, (c,), zl, k)
            _put(vz_ref, (c,), zl, v)
            _put(fz_ref, (c,), zl, ft)
            st8["q"].append(q); st8["k"].append(k); st8["v"].append(v)
        for j in range(6):
            sh = 1 << j
            for c in range(n_chunks):
                _put(scan_ref, (c,), lo,
                     _get(scan_ref, (c,), lo, A_CHUNK) + _get(scan_ref, (c,), lo - sh, A_CHUNK))
        for c in range(n_chunks):
            st8["b2"].append(_get(scan_ref, (c,), lo, A_CHUNK))
            st8["b2_last"].append(_get(scan_ref, (c,), lo + A_CHUNK - 1, 1))
        return st8

    def hgrn_pair_dots(s8):
        s8["levels"], s8["band"], s8["upd"] = [], [], []
        for c in range(n_chunks):
            q, k, v, b2 = s8["q"][c], s8["k"][c], s8["v"][c], s8["b2"][c]
            per_level = []
            for li, cs in enumerate(A_LEVELS):
                r_rows = [lo + (8 * r // cs) * cs - 1 for r in range(8)]
                e_rows = [lo + (8 * r // cs) * cs + cs - 1 for r in range(8)]
                qh = (q * jnp.exp2(b2 - bcast_rows(c, r_rows))).astype(BF16)
                kh = (k * jnp.exp2(bcast_rows(c, e_rows) - b2)).astype(BF16)
                per_level.append(_dot_nt(qh, block_diag(kh)))
            s8["levels"].append(per_level)
        for c in range(n_chunks):
            q = s8["q"][c]
            fprod = None
            for d in range(A_BAND):
                pd = q * _get(kz_ref, (c,), zl - d, A_CHUNK)
                if d >= 1:
                    fs = _get(fz_ref, (c,), zl - (d - 1), A_CHUNK)
                    fprod = fs if fprod is None else fprod * fs
                    pd = pd * fprod
                pall_ref[c, d * A_CHUNK:(d + 1) * A_CHUNK, :] = pd.astype(BF16)
            s8["band"].append(_dot(pall_ref[c], bd16))
        for c in range(n_chunks):
            khat = (s8["k"][c] * jnp.exp2(s8["b2_last"][c] - s8["b2"][c])).astype(BF16)
            s8["upd"].append(_dot_tn(s8["v"][c].astype(BF16), khat))

    def hgrn_outputs(s8):
        sts = [st_ref[...]]
        for c in range(n_chunks):
            sts.append(sts[c] * jnp.exp2(s8["b2_last"][c]) + s8["upd"][c] * bd32)
        st_ref[...] = sts[n_chunks]
        s8["o"] = []
        for c in range(n_chunks):
            q, v = s8["q"][c], s8["v"][c]
            scores = sum(s8["levels"][c][li] * lmask_ref[li] for li in range(n_lev))
            o = _dot_nt((q * jnp.exp2(s8["b2"][c])).astype(BF16), sts[c].astype(BF16))
            o = o + _dot(scores.astype(BF16), block_diag(v.astype(BF16)))
            for d in range(A_BAND):
                o = o + s8["band"][c][d * A_CHUNK:(d + 1) * A_CHUNK, :] * _get(vz_ref, (c,), zl - d, A_CHUNK)
            s8["o"].append(o)

    def hgrn_finish(s8):
        sq = []
        for c in range(n_chunks):
            o2 = s8["o"][c] * s8["o"][c]
            o2_hi = o2.astype(BF16)
            o2_lo = (o2 - o2_hi.astype(F32)).astype(BF16)
            sq.append(_dot(o2_hi, bd16) + _dot(o2_lo, bd16))
        for c in range(n_chunks):
            rows = chunk_rows(c)
            ms = sq[c] * (1.0 / A_HEAD_DIM)
            ya = s8["o"][c] * lax.rsqrt(ms + EPS) * a_norm_g * _silu(p_ref[rows, 3 * gw:4 * gw])
            ycat_ref[rows, 0:gw] = ya.astype(BF16)

    def conformer():
        _put(bbuf_ref, (), B_HALO, p_ref[:, o_b:o_b + gw] * _sigmoid(p_ref[:, o_b + gw:o_b + 2 * gw]))
        blk = MIX_BLOCK
        for rb in range(tt // blk):
            r0 = rb * blk
            base = B_HALO + r0 - (B_CONV_WIDTH - 1)
            acc = jnp.zeros((blk, gw), F32)
            for j in range(B_CONV_WIDTH):
                acc = acc + dww_ref[j:j + 1, :] * _get(bbuf_ref, (), base + j, blk)
            acc = acc + dwb_ref[...]
            mu = jnp.mean(acc, axis=-1, keepdims=True)
            cen = acc - mu
            var = jnp.mean(cen * cen, axis=-1, keepdims=True)
            yn = cen * lax.rsqrt(var + EPS) * lng_ref[...] + lnb_ref[...]
            yb = _dot(_silu(yn).astype(BF16), pww_ref[...]) + pwb_ref[...]
            ycat_ref[r0:r0 + blk, gw:2 * gw] = yb.astype(BF16)

    def short_conv():
        _put(cbuf_ref, (), C_HALO, p_ref[:, o_c + gw:o_c + 2 * gw] * p_ref[:, o_c + 2 * gw:o_c + 3 * gw])
        conv_c = jnp.zeros((tt, gw), F32)
        for j in range(C_CONV_WIDTH):
            off = C_HALO - (C_CONV_WIDTH - 1) + j
            conv_c = conv_c + ccw_ref[j:j + 1, :] * _get(cbuf_ref, (), off, tt)
        ycat_ref[:, 2 * gw:3 * gw] = (p_ref[:, o_c:o_c + gw] * conv_c).astype(BF16)

    def pooling():
        n = tt
        u = p_ref[:, o_d:o_d + gw]
        base = D_HALO
        _put(dbuf_ref, (), base, u)

        def window_sum(slab, w_small, w_big):
            s = dbuf_ref[slab, base:base + n, :]
            for j in range(1, w_small):
                s = s + dbuf_ref[slab, base - j:base - j + n, :]
            s_big = s
            for j in range(w_small, w_big):
                s_big = s_big + dbuf_ref[slab, base - j:base - j + n, :]
            lane = lax.broadcasted_iota(jnp.int32, (n, LANES), 1)
            return jnp.where(lane < LANES // 2, s, s_big)

        wsum = jnp.concatenate([window_sum(0, D_WINDOWS[0], D_WINDOWS[1]),
                                window_sum(1, D_WINDOWS[2], D_WINDOWS[3])], axis=1)
        pos1 = (ti * tt + 1 + lax.broadcasted_iota(jnp.int32, (n, gw), 0)).astype(F32)
        pooled = wsum / jnp.minimum(pos1, dwin_ref[...]) - u
        yd = _dot(pooled.astype(BF16), dproj_ref[...]) * dscale_ref[...]
        ycat_ref[:, 3 * gw:4 * gw] = yd.astype(BF16)

    h = _rms_norm(x_ref[0], preg_ref[...]).astype(BF16)
    for lo_c, hi_c in groups:
        p_ref[:, lo_c:hi_c] = _dot(h, w_in_ref[:, lo_c:hi_c])
    s8 = hgrn_prepare()
    hgrn_pair_dots(s8)
    hgrn_outputs(s8)
    hgrn_finish(s8)
    conformer()
    short_conv()
    pooling()
    y = _dot(ycat_ref[...], w_out_ref[...])
    o_ref[0] = x_ref[0] + _rms_norm(y, postg_ref[...])

    _put(bbuf_ref, (), 0, _get(bbuf_ref, (), tt, B_HALO))
    _put(cbuf_ref, (), 0, _get(cbuf_ref, (), tt, C_HALO))
    _put(dbuf_ref, (), 0, _get(dbuf_ref, (), tt, D_HALO))


def _const_spec(shape):
    nd = len(shape)
    return pl.BlockSpec(shape, lambda b, t, _nd=nd: (0,) * _nd)


def _mixer_call(layer, x, w_in, lb_gamma, a_norm_g, dw_w, dw_b, ln_g, ln_b, pw_w, pw_b, cc_w,
                dproj_bd, d_scale, w_out, pre_g, post_g, bd16, bd32, lmask, dwin):
    bsz, t_len, d = x.shape
    tt = MIX_TILE
    assert t_len % tt == 0 and tt % A_CHUNK == 0 and A_CHUNK % MIX_BLOCK == 0
    gw = GROUP_WIDTH
    consts = (w_in, lb_gamma, a_norm_g, dw_w, dw_b, ln_g, ln_b, pw_w, pw_b, cc_w, dproj_bd,
              d_scale, w_out, pre_g, post_g, bd16, bd32, lmask, dwin)
    tile_spec = pl.BlockSpec((1, tt, d), lambda b, t: (b, t, 0))
    slab = lambda rows: pltpu.VMEM((GROUP_SLABS, rows, LANES), F32)
    n_chunks = tt // A_CHUNK
    chunk_slab = lambda rows: pltpu.VMEM((n_chunks, GROUP_SLABS, rows, LANES), F32)
    return pl.pallas_call(
        functools.partial(_mixer_kernel, layer),
        name=f"mixer_l{layer}",
        grid=(bsz, t_len // tt),
        in_specs=[tile_spec] + [_const_spec(c.shape) for c in consts],
        out_specs=tile_spec,
        out_shape=jax.ShapeDtypeStruct(x.shape, x.dtype),
        scratch_shapes=[
            pltpu.VMEM((tt, IN_COLS), F32),
            pltpu.VMEM((tt, d), BF16),
            pltpu.VMEM((gw, gw), F32),
            chunk_slab(SCAN_HALO + A_CHUNK),
            chunk_slab(A_BAND + A_CHUNK),
            chunk_slab(A_BAND + A_CHUNK),
            chunk_slab(A_BAND + A_CHUNK),
            pltpu.VMEM((n_chunks, A_BAND * A_CHUNK, gw), BF16),
            slab(B_HALO + tt),
            slab(C_HALO + tt),
            slab(D_HALO + tt),
        ],
        compiler_params=pltpu.CompilerParams(
            dimension_semantics=("arbitrary", "arbitrary"),
            vmem_limit_bytes=VMEM_LIMIT_BYTES),
    )(x, *consts)


def _ffn_kernel(x_ref, w_up_ref, cw_ref, w_down_ref, preg_ref, postg_ref, o_ref,
                h_ref, ubuf_ref, act_ref):
    tt = x_ref.shape[1]
    ti = pl.program_id(1)
    half_slabs = FFN_CHUNK // LANES
    halves = (slice(0, tt // 2), slice(tt // 2, tt))

    @pl.when(ti == 0)
    def _reset():
        ubuf_ref[:, :, tt:tt + C_HALO, :] = jnp.zeros((N_FFN_CHUNKS, FFN_SLABS, C_HALO, LANES), F32)

    for rs in halves:
        h_ref[rs, :] = _rms_norm(x_ref[0, rs, :], preg_ref[...]).astype(BF16)

    def columns(c):
        return (c * FFN_CHUNK, D_FF + c * FFN_CHUNK)

    def up_project(c, row_slices):
        ubuf_ref[c, :, 0:C_HALO, :] = ubuf_ref[c, :, tt:tt + C_HALO, :]
        for rs in row_slices:
            for part, col0 in enumerate(columns(c)):
                u = _dot(h_ref[rs, :], w_up_ref[:, col0:col0 + FFN_CHUNK])
                _put(ubuf_ref, (c,), C_HALO + rs.start, u, slab0=part * half_slabs)

    def conv_act(c):
        for rs in halves:
            n = rs.stop - rs.start
            convs = []
            for part, col0 in enumerate(columns(c)):
                conv = jnp.zeros((n, FFN_CHUNK), F32)
                for j in range(FFN_CONV_WIDTH):
                    off = C_HALO - (FFN_CONV_WIDTH - 1) + j + rs.start
                    conv = conv + cw_ref[j:j + 1, col0:col0 + FFN_CHUNK] * _get(
                        ubuf_ref, (c,), off, n, slab0=part * half_slabs, slabs=half_slabs)
                convs.append(conv)
            act = _silu(convs[0]) * convs[1]
            act_ref[rs, c * FFN_CHUNK:(c + 1) * FFN_CHUNK] = act.astype(BF16)

    up_project(0, halves)
    for c in range(N_FFN_CHUNKS):
        if c + 1 < N_FFN_CHUNKS:
            up_project(c + 1, (slice(0, tt),))
        conv_act(c)

    for rs in halves:
        y = _dot(act_ref[rs, :], w_down_ref[...])
        o_ref[0, rs, :] = x_ref[0, rs, :] + _rms_norm(y, postg_ref[...])


def _ffn_call(x, w_up, conv_w, w_down, pre_g, post_g):
    bsz, t_len, d = x.shape
    tt = FFN_TILE
    assert t_len % tt == 0
    consts = (w_up, conv_w, w_down, pre_g, post_g)
    tile_spec = pl.BlockSpec((1, tt, d), lambda b, t: (b, t, 0))
    return pl.pallas_call(
        _ffn_kernel,
        name="ffn",
        grid=(bsz, t_len // tt),
        in_specs=[tile_spec] + [_const_spec(c.shape) for c in consts],
        out_specs=tile_spec,
        out_shape=jax.ShapeDtypeStruct(x.shape, x.dtype),
        scratch_shapes=[
            pltpu.VMEM((tt, d), BF16),
            pltpu.VMEM((N_FFN_CHUNKS, FFN_SLABS, C_HALO + tt, LANES), F32),
            pltpu.VMEM((tt, D_FF), BF16),
        ],
        compiler_params=pltpu.CompilerParams(
            dimension_semantics=("arbitrary", "arbitrary"),
            vmem_limit_bytes=VMEM_LIMIT_BYTES),
    )(x, *consts)


def _mixer_constants():
    hs = np.arange(GROUP_WIDTH)
    bd = (hs[:, None] // A_HEAD_DIM == hs[None, :] // A_HEAD_DIM).astype(np.float32)
    t = np.arange(A_CHUNK)[:, None]
    s = (np.arange(GROUP_WIDTH) % A_CHUNK)[None, :]
    lmask = np.stack([((t // c) % 2 == 1) & (s // c == t // c - 1) & (t - s >= A_BAND)
                      for c in A_LEVELS]).astype(np.float32)
    dwin = np.repeat(np.asarray(D_WINDOWS, np.float32), GROUP_WIDTH // len(D_WINDOWS))[None, :]
    return jnp.asarray(bd, BF16), jnp.asarray(bd), jnp.asarray(lmask), jnp.asarray(dwin)


def _block_diag(proj):
    g, n, _ = proj.shape
    out = jnp.zeros((g * n, g * n), proj.dtype)
    for i in range(g):
        out = out.at[i * n:(i + 1) * n, i * n:(i + 1) * n].set(proj[i])
    return out


def kernel(x, w_in, lb_gamma, a_norm_g, b_dw_w, b_dw_b, b_ln_g, b_ln_b, b_pw_w, b_pw_b, c_conv_w, d_proj, d_scale, w_out, mix_pre_g, mix_post_g, ffn_pre_g, ffn_post_g, w_up, ffn_conv_w, w_down):
    depth = w_in.shape[0]
    bd16, bd32, lmask, dwin = _mixer_constants()
    row = lambda a: a[None, :]
    for l in range(depth):
        x = _mixer_call(
            l, x, w_in[l].astype(BF16), lb_gamma, row(a_norm_g[l]), b_dw_w[l], row(b_dw_b[l]),
            row(b_ln_g[l]), row(b_ln_b[l]), b_pw_w[l].astype(BF16), row(b_pw_b[l]), c_conv_w[l],
            _block_diag(d_proj[l]).astype(BF16), row(d_scale[l]), w_out[l].astype(BF16),
            row(mix_pre_g[l]), row(mix_post_g[l]), bd16, bd32, lmask, dwin)
        x = _ffn_call(
            x, w_up[l].astype(BF16), ffn_conv_w[l], w_down[l].astype(BF16),
            row(ffn_pre_g[l]), row(ffn_post_g[l]))
    return x
```

```python
import functools

import jax
import jax.numpy as jnp
import numpy as np
from jax import lax
from jax.experimental import pallas as pl
from jax.experimental.pallas import tpu as pltpu

F32 = jnp.float32
BF16 = jnp.bfloat16

LANES = 128
D_MODEL = 1024
GROUP_WIDTH = 256
GROUP_SLABS = GROUP_WIDTH // LANES
A_HEADS = 4
A_HEAD_DIM = 64
A_CHUNK = 64
A_LEVELS = (32, 16, 8)
A_BAND = 8
B_CONV_WIDTH = 31
C_CONV_WIDTH = 3
D_WINDOWS = (2, 4, 8, 16)
D_FF = 2816
FFN_CONV_WIDTH = 3
IN_COLS = 10 * GROUP_WIDTH
EPS = 1e-6
MIN_FORGET = 1e-30

B_HALO = 32
C_HALO = 8
D_HALO = 16
SCAN_HALO = 32

MIX_TILE = 512
MIX_BLOCK = 64
FFN_TILE = 512
FFN_CHUNK = 256
FFN_SLABS = 2 * FFN_CHUNK // LANES
N_FFN_CHUNKS = D_FF // FFN_CHUNK
VMEM_LIMIT_BYTES = 56 * 1024 * 1024


_dot = functools.partial(jnp.dot, preferred_element_type=F32)
_dot_nt = functools.partial(lax.dot_general, dimension_numbers=(((1,), (1,)), ((), ())),
                            preferred_element_type=F32)
_dot_tn = functools.partial(lax.dot_general, dimension_numbers=(((0,), (0,)), ((), ())),
                            preferred_element_type=F32)


def _rms_norm(x, g):
    return x * lax.rsqrt(jnp.mean(x * x, axis=-1, keepdims=True) + EPS) * g


def _sigmoid(x):
    return 1.0 / (1.0 + jnp.exp(-x))


def _silu(x):
    return x * _sigmoid(x)


def _put(ref, lead, r0, val, slab0=0):
    n = val.shape[0]
    for l in range(val.shape[1] // LANES):
        ref[lead + (slab0 + l, slice(r0, r0 + n), slice(None))] = val[:, l * LANES:(l + 1) * LANES]


def _get(ref, lead, r0, n, slab0=0, slabs=None):
    slabs = ref.shape[len(lead)] - slab0 if slabs is None else slabs
    return jnp.concatenate(
        [ref[lead + (slab0 + l, slice(r0, r0 + n), slice(None))] for l in range(slabs)], axis=1)


def _mixer_kernel(layer, x_ref, w_in_ref, lbg_ref, ang_ref, dww_ref, dwb_ref, lng_ref, lnb_ref,
                  pww_ref, pwb_ref, ccw_ref, dproj_ref, dscale_ref, w_out_ref, preg_ref, postg_ref,
                  bd16_ref, bd32_ref, lmask_ref, dwin_ref,
                  o_ref,
                  p_ref, ycat_ref, st_ref, scan_ref, kz_ref, vz_ref, fz_ref, pall_ref,
                  bbuf_ref, cbuf_ref, dbuf_ref):
    tt = x_ref.shape[1]
    ti = pl.program_id(1)
    gw = GROUP_WIDTH
    n_chunks = tt // A_CHUNK
    n_lev = len(A_LEVELS)

    @pl.when(ti == 0)
    def _reset():
        st_ref[...] = jnp.zeros_like(st_ref)
        scan_ref[:, :, 0:SCAN_HALO, :] = jnp.zeros((n_chunks, GROUP_SLABS, SCAN_HALO, LANES), F32)
        kz_ref[:, :, 0:A_BAND, :] = jnp.zeros((n_chunks, GROUP_SLABS, A_BAND, LANES), F32)
        vz_ref[:, :, 0:A_BAND, :] = jnp.zeros((n_chunks, GROUP_SLABS, A_BAND, LANES), F32)
        fz_ref[:, :, 0:A_BAND, :] = jnp.zeros((n_chunks, GROUP_SLABS, A_BAND, LANES), F32)
        bbuf_ref[:, 0:B_HALO, :] = jnp.zeros((GROUP_SLABS, B_HALO, LANES), F32)
        cbuf_ref[:, 0:C_HALO, :] = jnp.zeros((GROUP_SLABS, C_HALO, LANES), F32)
        dbuf_ref[:, 0:D_HALO, :] = jnp.zeros((GROUP_SLABS, D_HALO, LANES), F32)

    lbg = lbg_ref[...]
    lbe = jnp.exp(lbg - jnp.max(lbg, axis=0, keepdims=True))
    lbs = lbe / jnp.sum(lbe, axis=0, keepdims=True)
    lb = jnp.sum(lbs[0:layer + 1, :], axis=0, keepdims=True) - lbs[0:1, :]
    one_m_lb = 1.0 - lb
    a_norm_g = ang_ref[...]
    bd16 = bd16_ref[...]
    bd32 = bd32_ref[...]
    groups = ((0, 4 * gw), (4 * gw, 6 * gw), (6 * gw, 9 * gw), (9 * gw, 10 * gw))
    o_b, o_c, o_d = 4 * gw, 6 * gw, 9 * gw
    lo = SCAN_HALO
    zl = A_BAND

    def chunk_rows(c):
        return slice(c * A_CHUNK, (c + 1) * A_CHUNK)

    def block_diag(val):
        return jnp.tile(val, (A_HEADS, 1)) * bd16

    def bcast_rows(c, row_ids):
        return jnp.concatenate(
            [jnp.broadcast_to(_get(scan_ref, (c,), r, 1), (8, gw)) for r in row_ids], axis=0)

    def hgrn_prepare():
        st8 = {"q": [], "k": [], "v": [], "b2": [], "b2_last": []}
        for c in range(n_chunks):
            rows = chunk_rows(c)
            z = p_ref[rows, gw:2 * gw]
            f = lb + one_m_lb * _sigmoid(z)
            ft = jnp.maximum(f, MIN_FORGET)
            k = 1.0 - f
            q = _silu(p_ref[rows, 0:gw]) * (A_HEAD_DIM ** -0.5)
            v = p_ref[rows, 2 * gw:3 * gw]
            _put(scan_ref, (c,), lo, jnp.log2(ft))
            _put(kz_ref, (c,), zl, k)
            _put(vz_ref, (c,), zl, v)
            _put(fz_ref, (c,), zl, ft)
            st8["q"].append(q); st8["k"].append(k); st8["v"].append(v)
        for j in range(6):
            sh = 1 << j
            for c in range(n_chunks):
                _put(scan_ref, (c,), lo,
                     _get(scan_ref, (c,), lo, A_CHUNK) + _get(scan_ref, (c,), lo - sh, A_CHUNK))
        for c in range(n_chunks):
            st8["b2"].append(_get(scan_ref, (c,), lo, A_CHUNK))
            st8["b2_last"].append(_get(scan_ref, (c,), lo + A_CHUNK - 1, 1))
        return st8

    def hgrn_pair_dots(s8):
        s8["levels"], s8["band"], s8["upd"] = [], [], []
        for c in range(n_chunks):
            q, k, v, b2 = s8["q"][c], s8["k"][c], s8["v"][c], s8["b2"][c]
            per_level = []
            for li, cs in enumerate(A_LEVELS):
                r_rows = [lo + (8 * r // cs) * cs - 1 for r in range(8)]
                e_rows = [lo + (8 * r // cs) * cs + cs - 1 for r in range(8)]
                qh = (q * jnp.exp2(b2 - bcast_rows(c, r_rows))).astype(BF16)
                kh = (k * jnp.exp2(bcast_rows(c, e_rows) - b2)).astype(BF16)
                per_level.append(_dot_nt(qh, block_diag(kh)))
            s8["levels"].append(per_level)
        for c in range(n_chunks):
            q = s8["q"][c]
            fprod = None
            for d in range(A_BAND):
                pd = q * _get(kz_ref, (c,), zl - d, A_CHUNK)
                if d >= 1:
                    fs = _get(fz_ref, (c,), zl - (d - 1), A_CHUNK)
                    fprod = fs if fprod is None else fprod * fs
                    pd = pd * fprod
                pall_ref[c, d * A_CHUNK:(d + 1) * A_CHUNK, :] = pd.astype(BF16)
            s8["band"].append(_dot(pall_ref[c], bd16))
        for c in range(n_chunks):
            khat = (s8["k"][c] * jnp.exp2(s8["b2_last"][c] - s8["b2"][c])).astype(BF16)
            s8["upd"].append(_dot_tn(s8["v"][c].astype(BF16), khat))

    def hgrn_outputs(s8):
        sts = [st_ref[...]]
        for c in range(n_chunks):
            sts.append(sts[c] * jnp.exp2(s8["b2_last"][c]) + s8["upd"][c] * bd32)
        st_ref[...] = sts[n_chunks]
        s8["o"] = []
        for c in range(n_chunks):
            q, v = s8["q"][c], s8["v"][c]
            scores = sum(s8["levels"][c][li] * lmask_ref[li] for li in range(n_lev))
            o = _dot_nt((q * jnp.exp2(s8["b2"][c])).astype(BF16), sts[c].astype(BF16))
            o = o + _dot(scores.astype(BF16), block_diag(v.astype(BF16)))
            for d in range(A_BAND):
                o = o + s8["band"][c][d * A_CHUNK:(d + 1) * A_CHUNK, :] * _get(vz_ref, (c,), zl - d, A_CHUNK)
            s8["o"].append(o)

    def hgrn_finish(s8):
        sq = []
        for c in range(n_chunks):
            o2 = s8["o"][c] * s8["o"][c]
            o2_hi = o2.astype(BF16)
            o2_lo = (o2 - o2_hi.astype(F32)).astype(BF16)
            sq.append(_dot(o2_hi, bd16) + _dot(o2_lo, bd16))
        for c in range(n_chunks):
            rows = chunk_rows(c)
            ms = sq[c] * (1.0 / A_HEAD_DIM)
            ya = s8["o"][c] * lax.rsqrt(ms + EPS) * a_norm_g * _silu(p_ref[rows, 3 * gw:4 * gw])
            ycat_ref[rows, 0:gw] = ya.astype(BF16)

    def conformer():
        _put(bbuf_ref, (), B_HALO, p_ref[:, o_b:o_b + gw] * _sigmoid(p_ref[:, o_b + gw:o_b + 2 * gw]))
        blk = MIX_BLOCK
        for rb in range(tt // blk):
            r0 = rb * blk
            base = B_HALO + r0 - (B_CONV_WIDTH - 1)
            acc = jnp.zeros((blk, gw), F32)
            for j in range(B_CONV_WIDTH):
                acc = acc + dww_ref[j:j + 1, :] * _get(bbuf_ref, (), base + j, blk)
            acc = acc + dwb_ref[...]
            mu = jnp.mean(acc, axis=-1, keepdims=True)
            cen = acc - mu
            var = jnp.mean(cen * cen, axis=-1, keepdims=True)
            yn = cen * lax.rsqrt(var + EPS) * lng_ref[...] + lnb_ref[...]
            yb = _dot(_silu(yn).astype(BF16), pww_ref[...]) + pwb_ref[...]
            ycat_ref[r0:r0 + blk, gw:2 * gw] = yb.astype(BF16)

    def short_conv():
        _put(cbuf_ref, (), C_HALO, p_ref[:, o_c + gw:o_c + 2 * gw] * p_ref[:, o_c + 2 * gw:o_c + 3 * gw])
        conv_c = jnp.zeros((tt, gw), F32)
        for j in range(C_CONV_WIDTH):
            off = C_HALO - (C_CONV_WIDTH - 1) + j
            conv_c = conv_c + ccw_ref[j:j + 1, :] * _get(cbuf_ref, (), off, tt)
        ycat_ref[:, 2 * gw:3 * gw] = (p_ref[:, o_c:o_c + gw] * conv_c).astype(BF16)

    def pooling():
        n = tt
        u = p_ref[:, o_d:o_d + gw]
        base = D_HALO
        _put(dbuf_ref, (), base, u)

        def window_sum(slab, w_small, w_big):
            s = dbuf_ref[slab, base:base + n, :]
            for j in range(1, w_small):
                s = s + dbuf_ref[slab, base - j:base - j + n, :]
            s_big = s
            for j in range(w_small, w_big):
                s_big = s_big + dbuf_ref[slab, base - j:base - j + n, :]
            lane = lax.broadcasted_iota(jnp.int32, (n, LANES), 1)
            return jnp.where(lane < LANES // 2, s, s_big)

        wsum = jnp.concatenate([window_sum(0, D_WINDOWS[0], D_WINDOWS[1]),
                                window_sum(1, D_WINDOWS[2], D_WINDOWS[3])], axis=1)
        pos1 = (ti * tt + 1 + lax.broadcasted_iota(jnp.int32, (n, gw), 0)).astype(F32)
        pooled = wsum / jnp.minimum(pos1, dwin_ref[...]) - u
        yd = _dot(pooled.astype(BF16), dproj_ref[...]) * dscale_ref[...]
        ycat_ref[:, 3 * gw:4 * gw] = yd.astype(BF16)

    h = _rms_norm(x_ref[0], preg_ref[...]).astype(BF16)
    for lo_c, hi_c in groups:
        p_ref[:, lo_c:hi_c] = _dot(h, w_in_ref[:, lo_c:hi_c])
    s8 = hgrn_prepare()
    hgrn_pair_dots(s8)
    hgrn_outputs(s8)
    hgrn_finish(s8)
    conformer()
    short_conv()
    pooling()
    y = _dot(ycat_ref[...], w_out_ref[...])
    o_ref[0] = x_ref[0] + _rms_norm(y, postg_ref[...])

    _put(bbuf_ref, (), 0, _get(bbuf_ref, (), tt, B_HALO))
    _put(cbuf_ref, (), 0, _get(cbuf_ref, (), tt, C_HALO))
    _put(dbuf_ref, (), 0, _get(dbuf_ref, (), tt, D_HALO))


def _const_spec(shape):
    nd = len(shape)
    return pl.BlockSpec(shape, lambda b, t, _nd=nd: (0,) * _nd)


def _layer_spec(shape, layer):
    nd = len(shape)
    return pl.BlockSpec((None,) + tuple(shape[1:]), lambda b, t, _nd=nd: (layer,) + (0,) * (_nd - 1))


def _mixer_call(layer, x, w_in, lb_gamma, a_norm_g, dw_w, dw_b, ln_g, ln_b, pw_w, pw_b, cc_w,
                dproj_bd, d_scale, w_out, pre_g, post_g, bd16, bd32, lmask, dwin):
    bsz, t_len, d = x.shape
    tt = MIX_TILE
    assert t_len % tt == 0 and tt % A_CHUNK == 0 and A_CHUNK % MIX_BLOCK == 0
    gw = GROUP_WIDTH
    consts = (w_in, lb_gamma, a_norm_g, dw_w, dw_b, ln_g, ln_b, pw_w, pw_b, cc_w, dproj_bd,
              d_scale, w_out, pre_g, post_g, bd16, bd32, lmask, dwin)
    shared = (lb_gamma, bd16, bd32, lmask, dwin)
    const_specs = [_const_spec(c.shape) if any(c is sh for sh in shared) else _layer_spec(c.shape, layer)
                   for c in consts]
    tile_spec = pl.BlockSpec((1, tt, d), lambda b, t: (b, t, 0))
    slab = lambda rows: pltpu.VMEM((GROUP_SLABS, rows, LANES), F32)
    n_chunks = tt // A_CHUNK
    chunk_slab = lambda rows: pltpu.VMEM((n_chunks, GROUP_SLABS, rows, LANES), F32)
    return pl.pallas_call(
        functools.partial(_mixer_kernel, layer),
        name=f"mixer_l{layer}",
        grid=(bsz, t_len // tt),
        in_specs=[tile_spec] + const_specs,
        out_specs=tile_spec,
        out_shape=jax.ShapeDtypeStruct(x.shape, x.dtype),
        scratch_shapes=[
            pltpu.VMEM((tt, IN_COLS), F32),
            pltpu.VMEM((tt, d), BF16),
            pltpu.VMEM((gw, gw), F32),
            chunk_slab(SCAN_HALO + A_CHUNK),
            chunk_slab(A_BAND + A_CHUNK),
            chunk_slab(A_BAND + A_CHUNK),
            chunk_slab(A_BAND + A_CHUNK),
            pltpu.VMEM((n_chunks, A_BAND * A_CHUNK, gw), BF16),
            slab(B_HALO + tt),
            slab(C_HALO + tt),
            slab(D_HALO + tt),
        ],
        compiler_params=pltpu.CompilerParams(
            dimension_semantics=("arbitrary", "arbitrary"),
            vmem_limit_bytes=VMEM_LIMIT_BYTES),
    )(x, *consts)


def _ffn_kernel(x_ref, w_up_ref, cw_ref, w_down_ref, preg_ref, postg_ref, o_ref,
                h_ref, ubuf_ref, act_ref):
    tt = x_ref.shape[1]
    ti = pl.program_id(1)
    half_slabs = FFN_CHUNK // LANES
    halves = (slice(0, tt // 2), slice(tt // 2, tt))

    @pl.when(ti == 0)
    def _reset():
        ubuf_ref[:, :, tt:tt + C_HALO, :] = jnp.zeros((N_FFN_CHUNKS, FFN_SLABS, C_HALO, LANES), F32)

    for rs in halves:
        h_ref[rs, :] = _rms_norm(x_ref[0, rs, :], preg_ref[...]).astype(BF16)

    def columns(c):
        return (c * FFN_CHUNK, D_FF + c * FFN_CHUNK)

    def up_project(c, row_slices):
        ubuf_ref[c, :, 0:C_HALO, :] = ubuf_ref[c, :, tt:tt + C_HALO, :]
        for rs in row_slices:
            for part, col0 in enumerate(columns(c)):
                u = _dot(h_ref[rs, :], w_up_ref[:, col0:col0 + FFN_CHUNK])
                _put(ubuf_ref, (c,), C_HALO + rs.start, u, slab0=part * half_slabs)

    def conv_act(c):
        for rs in halves:
            n = rs.stop - rs.start
            convs = []
            for part, col0 in enumerate(columns(c)):
                conv = jnp.zeros((n, FFN_CHUNK), F32)
                for j in range(FFN_CONV_WIDTH):
                    off = C_HALO - (FFN_CONV_WIDTH - 1) + j + rs.start
                    conv = conv + cw_ref[j:j + 1, col0:col0 + FFN_CHUNK] * _get(
                        ubuf_ref, (c,), off, n, slab0=part * half_slabs, slabs=half_slabs)
                convs.append(conv)
            act = _silu(convs[0]) * convs[1]
            act_ref[rs, c * FFN_CHUNK:(c + 1) * FFN_CHUNK] = act.astype(BF16)

    up_project(0, halves)
    for c in range(N_FFN_CHUNKS):
        if c + 1 < N_FFN_CHUNKS:
            up_project(c + 1, (slice(0, tt),))
        conv_act(c)

    for rs in halves:
        y = _dot(act_ref[rs, :], w_down_ref[...])
        o_ref[0, rs, :] = x_ref[0, rs, :] + _rms_norm(y, postg_ref[...])


def _ffn_call(layer, x, w_up, conv_w, w_down, pre_g, post_g):
    bsz, t_len, d = x.shape
    tt = FFN_TILE
    assert t_len % tt == 0
    consts = (w_up, conv_w, w_down, pre_g, post_g)
    tile_spec = pl.BlockSpec((1, tt, d), lambda b, t: (b, t, 0))
    return pl.pallas_call(
        _ffn_kernel,
        name="ffn",
        grid=(bsz, t_len // tt),
        in_specs=[tile_spec] + [_layer_spec(c.shape, layer) for c in consts],
        out_specs=tile_spec,
        out_shape=jax.ShapeDtypeStruct(x.shape, x.dtype),
        scratch_shapes=[
            pltpu.VMEM((tt, d), BF16),
            pltpu.VMEM((N_FFN_CHUNKS, FFN_SLABS, C_HALO + tt, LANES), F32),
            pltpu.VMEM((tt, D_FF), BF16),
        ],
        compiler_params=pltpu.CompilerParams(
            dimension_semantics=("arbitrary", "arbitrary"),
            vmem_limit_bytes=VMEM_LIMIT_BYTES),
    )(x, *consts)


def _mixer_constants():
    hs = np.arange(GROUP_WIDTH)
    bd = (hs[:, None] // A_HEAD_DIM == hs[None, :] // A_HEAD_DIM).astype(np.float32)
    t = np.arange(A_CHUNK)[:, None]
    s = (np.arange(GROUP_WIDTH) % A_CHUNK)[None, :]
    lmask = np.stack([((t // c) % 2 == 1) & (s // c == t // c - 1) & (t - s >= A_BAND)
                      for c in A_LEVELS]).astype(np.float32)
    dwin = np.repeat(np.asarray(D_WINDOWS, np.float32), GROUP_WIDTH // len(D_WINDOWS))[None, :]
    return jnp.asarray(bd, BF16), jnp.asarray(bd), jnp.asarray(lmask), jnp.asarray(dwin)


def _block_diag(proj):
    depth, g, n, _ = proj.shape
    out = jnp.zeros((depth, g * n, g * n), proj.dtype)
    for i in range(g):
        out = out.at[:, i * n:(i + 1) * n, i * n:(i + 1) * n].set(proj[:, i])
    return out


def kernel(x, w_in, lb_gamma, a_norm_g, b_dw_w, b_dw_b, b_ln_g, b_ln_b, b_pw_w, b_pw_b, c_conv_w, d_proj, d_scale, w_out, mix_pre_g, mix_post_g, ffn_pre_g, ffn_post_g, w_up, ffn_conv_w, w_down):
    depth = w_in.shape[0]
    bd16, bd32, lmask, dwin = _mixer_constants()
    rows = lambda a: a[:, None, :]
    mixer_params = (
        w_in.astype(BF16), lb_gamma, rows(a_norm_g), b_dw_w, rows(b_dw_b), rows(b_ln_g), rows(b_ln_b),
        b_pw_w.astype(BF16), rows(b_pw_b), c_conv_w, _block_diag(d_proj).astype(BF16), rows(d_scale),
        w_out.astype(BF16), rows(mix_pre_g), rows(mix_post_g), bd16, bd32, lmask, dwin)
    ffn_params = (w_up.astype(BF16), ffn_conv_w, w_down.astype(BF16), rows(ffn_pre_g), rows(ffn_post_g))
    for l in range(depth):
        x = _mixer_call(l, x, *mixer_params)
        x = _ffn_call(l, x, *ffn_params)
    return x
```

```python
import functools

import jax
import jax.numpy as jnp
import numpy as np
from jax import lax
from jax.experimental import pallas as pl
from jax.experimental.pallas import tpu as pltpu

F32 = jnp.float32
BF16 = jnp.bfloat16

LANES = 128
D_MODEL = 1024
GROUP_WIDTH = 256
GROUP_SLABS = GROUP_WIDTH // LANES
A_HEADS = 4
A_HEAD_DIM = 64
A_CHUNK = 64
A_LEVELS = (32, 16, 8)
A_BAND = 8
B_CONV_WIDTH = 31
C_CONV_WIDTH = 3
D_WINDOWS = (2, 4, 8, 16)
D_FF = 2816
FFN_CONV_WIDTH = 3
IN_COLS = 10 * GROUP_WIDTH
EPS = 1e-6
MIN_FORGET = 1e-30

B_HALO = 32
C_HALO = 8
D_HALO = 16
SCAN_HALO = 32

MIX_TILE = 1024
MIX_BLOCK = 64
FFN_TILE = 512
FFN_CHUNK = 256
FFN_SLABS = 2 * FFN_CHUNK // LANES
N_FFN_CHUNKS = D_FF // FFN_CHUNK
VMEM_LIMIT_BYTES = 56 * 1024 * 1024


_dot = functools.partial(jnp.dot, preferred_element_type=F32)
_dot_nt = functools.partial(lax.dot_general, dimension_numbers=(((1,), (1,)), ((), ())),
                            preferred_element_type=F32)
_dot_tn = functools.partial(lax.dot_general, dimension_numbers=(((0,), (0,)), ((), ())),
                            preferred_element_type=F32)


def _rms_norm(x, g):
    return x * lax.rsqrt(jnp.mean(x * x, axis=-1, keepdims=True) + EPS) * g


def _sigmoid(x):
    return 1.0 / (1.0 + jnp.exp(-x))


def _silu(x):
    return x * _sigmoid(x)


def _put(ref, lead, r0, val, slab0=0):
    n = val.shape[0]
    for l in range(val.shape[1] // LANES):
        ref[lead + (slab0 + l, slice(r0, r0 + n), slice(None))] = val[:, l * LANES:(l + 1) * LANES]


def _get(ref, lead, r0, n, slab0=0, slabs=None):
    slabs = ref.shape[len(lead)] - slab0 if slabs is None else slabs
    return jnp.concatenate(
        [ref[lead + (slab0 + l, slice(r0, r0 + n), slice(None))] for l in range(slabs)], axis=1)


def _mixer_kernel(layer, x_ref, w_in_ref, lbg_ref, ang_ref, dww_ref, dwb_ref, lng_ref, lnb_ref,
                  pww_ref, pwb_ref, ccw_ref, dproj_ref, dscale_ref, w_out_ref, preg_ref, postg_ref,
                  bd16_ref, bd32_ref, lmask_ref, dwin_ref,
                  o_ref,
                  p_ref, ycat_ref, st_ref, scan_ref, kz_ref, vz_ref, fz_ref, pall_ref,
                  bbuf_ref, cbuf_ref, dbuf_ref):
    tt = x_ref.shape[1]
    ti = pl.program_id(1)
    gw = GROUP_WIDTH
    n_chunks = tt // A_CHUNK
    n_lev = len(A_LEVELS)

    @pl.when(ti == 0)
    def _reset():
        st_ref[...] = jnp.zeros_like(st_ref)
        scan_ref[:, :, 0:SCAN_HALO, :] = jnp.zeros((n_chunks, GROUP_SLABS, SCAN_HALO, LANES), F32)
        kz_ref[:, :, 0:A_BAND, :] = jnp.zeros((n_chunks, GROUP_SLABS, A_BAND, LANES), F32)
        vz_ref[:, :, 0:A_BAND, :] = jnp.zeros((n_chunks, GROUP_SLABS, A_BAND, LANES), F32)
        fz_ref[:, :, 0:A_BAND, :] = jnp.zeros((n_chunks, GROUP_SLABS, A_BAND, LANES), F32)
        bbuf_ref[:, 0:B_HALO, :] = jnp.zeros((GROUP_SLABS, B_HALO, LANES), F32)
        cbuf_ref[:, 0:C_HALO, :] = jnp.zeros((GROUP_SLABS, C_HALO, LANES), F32)
        dbuf_ref[:, 0:D_HALO, :] = jnp.zeros((GROUP_SLABS, D_HALO, LANES), F32)

    lbg = lbg_ref[...]
    lbe = jnp.exp(lbg - jnp.max(lbg, axis=0, keepdims=True))
    lbs = lbe / jnp.sum(lbe, axis=0, keepdims=True)
    lb = jnp.sum(lbs[0:layer + 1, :], axis=0, keepdims=True) - lbs[0:1, :]
    one_m_lb = 1.0 - lb
    a_norm_g = ang_ref[...]
    bd16 = bd16_ref[...]
    bd32 = bd32_ref[...]
    groups = ((0, 4 * gw), (4 * gw, 6 * gw), (6 * gw, 9 * gw), (9 * gw, 10 * gw))
    o_b, o_c, o_d = 4 * gw, 6 * gw, 9 * gw
    lo = SCAN_HALO
    zl = A_BAND

    def chunk_rows(c):
        return slice(c * A_CHUNK, (c + 1) * A_CHUNK)

    def block_diag(val):
        return jnp.tile(val, (A_HEADS, 1)) * bd16

    def bcast_rows(c, row_ids):
        return jnp.concatenate(
            [jnp.broadcast_to(_get(scan_ref, (c,), r, 1), (8, gw)) for r in row_ids], axis=0)

    def hgrn_prepare():
        st8 = {"q": [], "k": [], "v": [], "b2": [], "b2_last": []}
        for c in range(n_chunks):
            rows = chunk_rows(c)
            z = p_ref[rows, gw:2 * gw]
            f = lb + one_m_lb * _sigmoid(z)
            ft = jnp.maximum(f, MIN_FORGET)
            k = 1.0 - f
            q = _silu(p_ref[rows, 0:gw]) * (A_HEAD_DIM ** -0.5)
            v = p_ref[rows, 2 * gw:3 * gw]
            _put(scan_ref, (c,), lo, jnp.log2(ft))
            _put(kz_ref, (c,), zl, k)
            _put(vz_ref, (c,), zl, v)
            _put(fz_ref, (c,), zl, ft)
            st8["q"].append(q); st8["k"].append(k); st8["v"].append(v)
        for j in range(6):
            sh = 1 << j
            for c in range(n_chunks):
                _put(scan_ref, (c,), lo,
                     _get(scan_ref, (c,), lo, A_CHUNK) + _get(scan_ref, (c,), lo - sh, A_CHUNK))
        for c in range(n_chunks):
            st8["b2"].append(_get(scan_ref, (c,), lo, A_CHUNK))
            st8["b2_last"].append(_get(scan_ref, (c,), lo + A_CHUNK - 1, 1))
        return st8

    def hgrn_pair_dots(s8):
        s8["levels"], s8["band"], s8["upd"] = [], [], []
        for c in range(n_chunks):
            q, k, v, b2 = s8["q"][c], s8["k"][c], s8["v"][c], s8["b2"][c]
            per_level = []
            for li, cs in enumerate(A_LEVELS):
                r_rows = [lo + (8 * r // cs) * cs - 1 for r in range(8)]
                e_rows = [lo + (8 * r // cs) * cs + cs - 1 for r in range(8)]
                qh = (q * jnp.exp2(b2 - bcast_rows(c, r_rows))).astype(BF16)
                kh = (k * jnp.exp2(bcast_rows(c, e_rows) - b2)).astype(BF16)
                per_level.append(_dot_nt(qh, block_diag(kh)))
            s8["levels"].append(per_level)
        for c in range(n_chunks):
            q = s8["q"][c]
            fprod = None
            for d in range(A_BAND):
                pd = q * _get(kz_ref, (c,), zl - d, A_CHUNK)
                if d >= 1:
                    fs = _get(fz_ref, (c,), zl - (d - 1), A_CHUNK)
                    fprod = fs if fprod is None else fprod * fs
                    pd = pd * fprod
                pall_ref[c, d * A_CHUNK:(d + 1) * A_CHUNK, :] = pd.astype(BF16)
            s8["band"].append(_dot(pall_ref[c], bd16))
        for c in range(n_chunks):
            khat = (s8["k"][c] * jnp.exp2(s8["b2_last"][c] - s8["b2"][c])).astype(BF16)
            s8["upd"].append(_dot_tn(s8["v"][c].astype(BF16), khat))

    def hgrn_outputs(s8):
        sts = [st_ref[...]]
        for c in range(n_chunks):
            sts.append(sts[c] * jnp.exp2(s8["b2_last"][c]) + s8["upd"][c] * bd32)
        st_ref[...] = sts[n_chunks]
        s8["o"] = []
        for c in range(n_chunks):
            q, v = s8["q"][c], s8["v"][c]
            scores = sum(s8["levels"][c][li] * lmask_ref[li] for li in range(n_lev))
            o = _dot_nt((q * jnp.exp2(s8["b2"][c])).astype(BF16), sts[c].astype(BF16))
            o = o + _dot(scores.astype(BF16), block_diag(v.astype(BF16)))
            for d in range(A_BAND):
                o = o + s8["band"][c][d * A_CHUNK:(d + 1) * A_CHUNK, :] * _get(vz_ref, (c,), zl - d, A_CHUNK)
            s8["o"].append(o)

    def hgrn_finish(s8):
        sq = []
        for c in range(n_chunks):
            o2 = s8["o"][c] * s8["o"][c]
            o2_hi = o2.astype(BF16)
            o2_lo = (o2 - o2_hi.astype(F32)).astype(BF16)
            sq.append(_dot(o2_hi, bd16) + _dot(o2_lo, bd16))
        for c in range(n_chunks):
            rows = chunk_rows(c)
            ms = sq[c] * (1.0 / A_HEAD_DIM)
            ya = s8["o"][c] * lax.rsqrt(ms + EPS) * a_norm_g * _silu(p_ref[rows, 3 * gw:4 * gw])
            ycat_ref[rows, 0:gw] = ya.astype(BF16)

    def conformer():
        _put(bbuf_ref, (), B_HALO, p_ref[:, o_b:o_b + gw] * _sigmoid(p_ref[:, o_b + gw:o_b + 2 * gw]))
        blk = MIX_BLOCK
        for rb in range(tt // blk):
            r0 = rb * blk
            base = B_HALO + r0 - (B_CONV_WIDTH - 1)
            acc = jnp.zeros((blk, gw), F32)
            for j in range(B_CONV_WIDTH):
                acc = acc + dww_ref[j:j + 1, :] * _get(bbuf_ref, (), base + j, blk)
            acc = acc + dwb_ref[...]
            mu = jnp.mean(acc, axis=-1, keepdims=True)
            cen = acc - mu
            var = jnp.mean(cen * cen, axis=-1, keepdims=True)
            yn = cen * lax.rsqrt(var + EPS) * lng_ref[...] + lnb_ref[...]
            yb = _dot(_silu(yn).astype(BF16), pww_ref[...]) + pwb_ref[...]
            ycat_ref[r0:r0 + blk, gw:2 * gw] = yb.astype(BF16)

    def short_conv():
        _put(cbuf_ref, (), C_HALO, p_ref[:, o_c + gw:o_c + 2 * gw] * p_ref[:, o_c + 2 * gw:o_c + 3 * gw])
        conv_c = jnp.zeros((tt, gw), F32)
        for j in range(C_CONV_WIDTH):
            off = C_HALO - (C_CONV_WIDTH - 1) + j
            conv_c = conv_c + ccw_ref[j:j + 1, :] * _get(cbuf_ref, (), off, tt)
        ycat_ref[:, 2 * gw:3 * gw] = (p_ref[:, o_c:o_c + gw] * conv_c).astype(BF16)

    def pooling():
        n = tt
        u = p_ref[:, o_d:o_d + gw]
        base = D_HALO
        _put(dbuf_ref, (), base, u)

        def window_sum(slab, w_small, w_big):
            s = dbuf_ref[slab, base:base + n, :]
            for j in range(1, w_small):
                s = s + dbuf_ref[slab, base - j:base - j + n, :]
            s_big = s
            for j in range(w_small, w_big):
                s_big = s_big + dbuf_ref[slab, base - j:base - j + n, :]
            lane = lax.broadcasted_iota(jnp.int32, (n, LANES), 1)
            return jnp.where(lane < LANES // 2, s, s_big)

        wsum = jnp.concatenate([window_sum(0, D_WINDOWS[0], D_WINDOWS[1]),
                                window_sum(1, D_WINDOWS[2], D_WINDOWS[3])], axis=1)
        pos1 = (ti * tt + 1 + lax.broadcasted_iota(jnp.int32, (n, gw), 0)).astype(F32)
        pooled = wsum / jnp.minimum(pos1, dwin_ref[...]) - u
        yd = _dot(pooled.astype(BF16), dproj_ref[...]) * dscale_ref[...]
        ycat_ref[:, 3 * gw:4 * gw] = yd.astype(BF16)

    h = _rms_norm(x_ref[0], preg_ref[...]).astype(BF16)
    for lo_c, hi_c in groups:
        p_ref[:, lo_c:hi_c] = _dot(h, w_in_ref[:, lo_c:hi_c])
    s8 = hgrn_prepare()
    hgrn_pair_dots(s8)
    hgrn_outputs(s8)
    hgrn_finish(s8)
    conformer()
    short_conv()
    pooling()
    y = _dot(ycat_ref[...], w_out_ref[...])
    o_ref[0] = x_ref[0] + _rms_norm(y, postg_ref[...])

    _put(bbuf_ref, (), 0, _get(bbuf_ref, (), tt, B_HALO))
    _put(cbuf_ref, (), 0, _get(cbuf_ref, (), tt, C_HALO))
    _put(dbuf_ref, (), 0, _get(dbuf_ref, (), tt, D_HALO))


def _const_spec(shape):
    nd = len(shape)
    return pl.BlockSpec(shape, lambda b, t, _nd=nd: (0,) * _nd)


def _layer_spec(shape, layer):
    nd = len(shape)
    return pl.BlockSpec((None,) + tuple(shape[1:]), lambda b, t, _nd=nd: (layer,) + (0,) * (_nd - 1))


def _mixer_call(layer, x, w_in, lb_gamma, a_norm_g, dw_w, dw_b, ln_g, ln_b, pw_w, pw_b, cc_w,
                dproj_bd, d_scale, w_out, pre_g, post_g, bd16, bd32, lmask, dwin):
    bsz, t_len, d = x.shape
    tt = MIX_TILE
    assert t_len % tt == 0 and tt % A_CHUNK == 0 and A_CHUNK % MIX_BLOCK == 0
    gw = GROUP_WIDTH
    consts = (w_in, lb_gamma, a_norm_g, dw_w, dw_b, ln_g, ln_b, pw_w, pw_b, cc_w, dproj_bd,
              d_scale, w_out, pre_g, post_g, bd16, bd32, lmask, dwin)
    shared = (lb_gamma, bd16, bd32, lmask, dwin)
    const_specs = [_const_spec(c.shape) if any(c is sh for sh in shared) else _layer_spec(c.shape, layer)
                   for c in consts]
    tile_spec = pl.BlockSpec((1, tt, d), lambda b, t: (b, t, 0))
    slab = lambda rows: pltpu.VMEM((GROUP_SLABS, rows, LANES), F32)
    n_chunks = tt // A_CHUNK
    chunk_slab = lambda rows: pltpu.VMEM((n_chunks, GROUP_SLABS, rows, LANES), F32)
    return pl.pallas_call(
        functools.partial(_mixer_kernel, layer),
        name=f"mixer_l{layer}",
        grid=(bsz, t_len // tt),
        in_specs=[tile_spec] + const_specs,
        out_specs=tile_spec,
        out_shape=jax.ShapeDtypeStruct(x.shape, x.dtype),
        scratch_shapes=[
            pltpu.VMEM((tt, IN_COLS), F32),
            pltpu.VMEM((tt, d), BF16),
            pltpu.VMEM((gw, gw), F32),
            chunk_slab(SCAN_HALO + A_CHUNK),
            chunk_slab(A_BAND + A_CHUNK),
            chunk_slab(A_BAND + A_CHUNK),
            chunk_slab(A_BAND + A_CHUNK),
            pltpu.VMEM((n_chunks, A_BAND * A_CHUNK, gw), BF16),
            slab(B_HALO + tt),
            slab(C_HALO + tt),
            slab(D_HALO + tt),
        ],
        compiler_params=pltpu.CompilerParams(
            dimension_semantics=("arbitrary", "arbitrary"),
            vmem_limit_bytes=VMEM_LIMIT_BYTES),
    )(x, *consts)


def _ffn_kernel(x_ref, w_up_ref, cw_ref, w_down_ref, preg_ref, postg_ref, o_ref,
                h_ref, ubuf_ref, act_ref):
    tt = x_ref.shape[1]
    ti = pl.program_id(1)
    half_slabs = FFN_CHUNK // LANES
    halves = (slice(0, tt // 2), slice(tt // 2, tt))

    @pl.when(ti == 0)
    def _reset():
        ubuf_ref[:, :, tt:tt + C_HALO, :] = jnp.zeros((N_FFN_CHUNKS, FFN_SLABS, C_HALO, LANES), F32)

    for rs in halves:
        h_ref[rs, :] = _rms_norm(x_ref[0, rs, :], preg_ref[...]).astype(BF16)

    def columns(c):
        return (c * FFN_CHUNK, D_FF + c * FFN_CHUNK)

    def up_project(c, row_slices):
        ubuf_ref[c, :, 0:C_HALO, :] = ubuf_ref[c, :, tt:tt + C_HALO, :]
        for rs in row_slices:
            for part, col0 in enumerate(columns(c)):
                u = _dot(h_ref[rs, :], w_up_ref[:, col0:col0 + FFN_CHUNK])
                _put(ubuf_ref, (c,), C_HALO + rs.start, u, slab0=part * half_slabs)

    def conv_act(c):
        for rs in halves:
            n = rs.stop - rs.start
            convs = []
            for part, col0 in enumerate(columns(c)):
                conv = jnp.zeros((n, FFN_CHUNK), F32)
                for j in range(FFN_CONV_WIDTH):
                    off = C_HALO - (FFN_CONV_WIDTH - 1) + j + rs.start
                    conv = conv + cw_ref[j:j + 1, col0:col0 + FFN_CHUNK] * _get(
                        ubuf_ref, (c,), off, n, slab0=part * half_slabs, slabs=half_slabs)
                convs.append(conv)
            act = _silu(convs[0]) * convs[1]
            act_ref[rs, c * FFN_CHUNK:(c + 1) * FFN_CHUNK] = act.astype(BF16)

    up_project(0, halves)
    for c in range(N_FFN_CHUNKS):
        if c + 1 < N_FFN_CHUNKS:
            up_project(c + 1, (slice(0, tt),))
        conv_act(c)

    for rs in halves:
        y = _dot(act_ref[rs, :], w_down_ref[...])
        o_ref[0, rs, :] = x_ref[0, rs, :] + _rms_norm(y, postg_ref[...])


def _ffn_call(layer, x, w_up, conv_w, w_down, pre_g, post_g):
    bsz, t_len, d = x.shape
    tt = FFN_TILE
    assert t_len % tt == 0
    consts = (w_up, conv_w, w_down, pre_g, post_g)
    tile_spec = pl.BlockSpec((1, tt, d), lambda b, t: (b, t, 0))
    return pl.pallas_call(
        _ffn_kernel,
        name="ffn",
        grid=(bsz, t_len // tt),
        in_specs=[tile_spec] + [_layer_spec(c.shape, layer) for c in consts],
        out_specs=tile_spec,
        out_shape=jax.ShapeDtypeStruct(x.shape, x.dtype),
        scratch_shapes=[
            pltpu.VMEM((tt, d), BF16),
            pltpu.VMEM((N_FFN_CHUNKS, FFN_SLABS, C_HALO + tt, LANES), F32),
            pltpu.VMEM((tt, D_FF), BF16),
        ],
        compiler_params=pltpu.CompilerParams(
            dimension_semantics=("arbitrary", "arbitrary"),
            vmem_limit_bytes=VMEM_LIMIT_BYTES),
    )(x, *consts)


def _mixer_constants():
    hs = np.arange(GROUP_WIDTH)
    bd = (hs[:, None] // A_HEAD_DIM == hs[None, :] // A_HEAD_DIM).astype(np.float32)
    t = np.arange(A_CHUNK)[:, None]
    s = (np.arange(GROUP_WIDTH) % A_CHUNK)[None, :]
    lmask = np.stack([((t // c) % 2 == 1) & (s // c == t // c - 1) & (t - s >= A_BAND)
                      for c in A_LEVELS]).astype(np.float32)
    dwin = np.repeat(np.asarray(D_WINDOWS, np.float32), GROUP_WIDTH // len(D_WINDOWS))[None, :]
    return jnp.asarray(bd, BF16), jnp.asarray(bd), jnp.asarray(lmask), jnp.asarray(dwin)


def _block_diag(proj):
    depth, g, n, _ = proj.shape
    out = jnp.zeros((depth, g * n, g * n), proj.dtype)
    for i in range(g):
        out = out.at[:, i * n:(i + 1) * n, i * n:(i + 1) * n].set(proj[:, i])
    return out


def kernel(x, w_in, lb_gamma, a_norm_g, b_dw_w, b_dw_b, b_ln_g, b_ln_b, b_pw_w, b_pw_b, c_conv_w, d_proj, d_scale, w_out, mix_pre_g, mix_post_g, ffn_pre_g, ffn_post_g, w_up, ffn_conv_w, w_down):
    depth = w_in.shape[0]
    bd16, bd32, lmask, dwin = _mixer_constants()
    rows = lambda a: a[:, None, :]
    mixer_params = (
        w_in.astype(BF16), lb_gamma, rows(a_norm_g), b_dw_w, rows(b_dw_b), rows(b_ln_g), rows(b_ln_b),
        b_pw_w.astype(BF16), rows(b_pw_b), c_conv_w, _block_diag(d_proj).astype(BF16), rows(d_scale),
        w_out.astype(BF16), rows(mix_pre_g), rows(mix_post_g), bd16, bd32, lmask, dwin)
    ffn_params = (w_up.astype(BF16), ffn_conv_w, w_down.astype(BF16), rows(ffn_pre_g), rows(ffn_post_g))
    for l in range(depth):
        x = _mixer_call(l, x, *mixer_params)
        x = _ffn_call(l, x, *ffn_params)
    return x
```

```python
import functools

import jax
import jax.numpy as jnp
import numpy as np
from jax import lax
from jax.experimental import pallas as pl
from jax.experimental.pallas import tpu as pltpu

F32 = jnp.float32
BF16 = jnp.bfloat16

LANES = 128
D_MODEL = 1024
GROUP_WIDTH = 256
GROUP_SLABS = GROUP_WIDTH // LANES
A_HEADS = 4
A_HEAD_DIM = 64
A_CHUNK = 64
A_LEVELS = (32, 16, 8)
A_BAND = 8
B_CONV_WIDTH = 31
C_CONV_WIDTH = 3
D_WINDOWS = (2, 4, 8, 16)
D_FF = 2816
FFN_CONV_WIDTH = 3
IN_COLS = 10 * GROUP_WIDTH
EPS = 1e-6
MIN_FORGET = 1e-30

B_HALO = 32
C_HALO = 8
D_HALO = 16
SCAN_HALO = 32

MIX_TILE = 1024
MIX_BLOCK = 64
FFN_TILE = 512
FFN_CHUNK = 256
FFN_SLABS = 2 * FFN_CHUNK // LANES
N_FFN_CHUNKS = D_FF // FFN_CHUNK
VMEM_LIMIT_BYTES = 56 * 1024 * 1024


_dot = functools.partial(jnp.dot, preferred_element_type=F32)
_dot_nt = functools.partial(lax.dot_general, dimension_numbers=(((1,), (1,)), ((), ())),
                            preferred_element_type=F32)
_dot_tn = functools.partial(lax.dot_general, dimension_numbers=(((0,), (0,)), ((), ())),
                            preferred_element_type=F32)


def _rms_norm(x, g):
    return x * lax.rsqrt(jnp.mean(x * x, axis=-1, keepdims=True) + EPS) * g


def _sigmoid(x):
    return 1.0 / (1.0 + jnp.exp(-x))


def _silu(x):
    return x * _sigmoid(x)


def _put(ref, lead, r0, val, slab0=0):
    n = val.shape[0]
    for l in range(val.shape[1] // LANES):
        ref[lead + (slab0 + l, slice(r0, r0 + n), slice(None))] = val[:, l * LANES:(l + 1) * LANES]


def _get(ref, lead, r0, n, slab0=0, slabs=None):
    slabs = ref.shape[len(lead)] - slab0 if slabs is None else slabs
    return jnp.concatenate(
        [ref[lead + (slab0 + l, slice(r0, r0 + n), slice(None))] for l in range(slabs)], axis=1)


def _mixer_kernel(layer, x_ref, w_in_ref, lbg_ref, ang_ref, dww_ref, dwb_ref, lng_ref, lnb_ref,
                  pww_ref, pwb_ref, ccw_ref, dproj_ref, dscale_ref, w_out_ref, preg_ref, postg_ref,
                  bd16_ref, bd32_ref, lmask_ref, dwin_ref,
                  o_ref,
                  p_ref, ycat_ref, st_ref, scan_ref, kz_ref, vz_ref, fz_ref, pall_ref,
                  bbuf_ref, cbuf_ref, dbuf_ref):
    tt = x_ref.shape[1]
    ti = pl.program_id(1)
    gw = GROUP_WIDTH
    n_chunks = tt // A_CHUNK
    n_lev = len(A_LEVELS)

    @pl.when(ti == 0)
    def _reset():
        st_ref[...] = jnp.zeros_like(st_ref)
        scan_ref[:, :, 0:SCAN_HALO, :] = jnp.zeros((n_chunks, GROUP_SLABS, SCAN_HALO, LANES), F32)
        kz_ref[:, :, 0:A_BAND, :] = jnp.zeros((n_chunks, GROUP_SLABS, A_BAND, LANES), F32)
        vz_ref[:, :, 0:A_BAND, :] = jnp.zeros((n_chunks, GROUP_SLABS, A_BAND, LANES), F32)
        fz_ref[:, :, 0:A_BAND, :] = jnp.zeros((n_chunks, GROUP_SLABS, A_BAND, LANES), F32)
        bbuf_ref[:, 0:B_HALO, :] = jnp.zeros((GROUP_SLABS, B_HALO, LANES), F32)
        cbuf_ref[:, 0:C_HALO, :] = jnp.zeros((GROUP_SLABS, C_HALO, LANES), F32)
        dbuf_ref[:, 0:D_HALO, :] = jnp.zeros((GROUP_SLABS, D_HALO, LANES), F32)

    lbg = lbg_ref[...]
    lbe = jnp.exp(lbg - jnp.max(lbg, axis=0, keepdims=True))
    lbs = lbe / jnp.sum(lbe, axis=0, keepdims=True)
    lb = jnp.sum(lbs[0:layer + 1, :], axis=0, keepdims=True) - lbs[0:1, :]
    one_m_lb = 1.0 - lb
    a_norm_g = ang_ref[...]
    bd16 = bd16_ref[...]
    bd32 = bd32_ref[...]
    groups = ((0, 4 * gw), (4 * gw, 6 * gw), (6 * gw, 9 * gw), (9 * gw, 10 * gw))
    o_b, o_c, o_d = 4 * gw, 6 * gw, 9 * gw
    lo = SCAN_HALO
    zl = A_BAND

    def chunk_rows(c):
        return slice(c * A_CHUNK, (c + 1) * A_CHUNK)

    def block_diag(val):
        return jnp.tile(val, (A_HEADS, 1)) * bd16

    def bcast_rows(c, row_ids):
        return jnp.concatenate(
            [jnp.broadcast_to(_get(scan_ref, (c,), r, 1), (8, gw)) for r in row_ids], axis=0)

    def hgrn_prepare():
        st8 = {"q": [], "k": [], "v": [], "b2": [], "b2_last": []}
        for c in range(n_chunks):
            rows = chunk_rows(c)
            z = p_ref[rows, gw:2 * gw]
            f = lb + one_m_lb * _sigmoid(z)
            ft = jnp.maximum(f, MIN_FORGET)
            k = 1.0 - f
            q = _silu(p_ref[rows, 0:gw]) * (A_HEAD_DIM ** -0.5)
            v = p_ref[rows, 2 * gw:3 * gw]
            _put(scan_ref, (c,), lo, jnp.log2(ft))
            _put(kz_ref, (c,), zl, k)
            _put(vz_ref, (c,), zl, v)
            _put(fz_ref, (c,), zl, ft)
            st8["q"].append(q); st8["k"].append(k); st8["v"].append(v)
        for j in range(6):
            sh = 1 << j
            for c in range(n_chunks):
                _put(scan_ref, (c,), lo,
                     _get(scan_ref, (c,), lo, A_CHUNK) + _get(scan_ref, (c,), lo - sh, A_CHUNK))
        for c in range(n_chunks):
            st8["b2"].append(_get(scan_ref, (c,), lo, A_CHUNK))
            st8["b2_last"].append(_get(scan_ref, (c,), lo + A_CHUNK - 1, 1))
        return st8

    def hgrn_pair_dots(s8):
        s8["levels"], s8["band"], s8["upd"] = [], [], []
        for c in range(n_chunks):
            q, k, v, b2 = s8["q"][c], s8["k"][c], s8["v"][c], s8["b2"][c]
            per_level = []
            for li, cs in enumerate(A_LEVELS):
                r_rows = [lo + (8 * r // cs) * cs - 1 for r in range(8)]
                e_rows = [lo + (8 * r // cs) * cs + cs - 1 for r in range(8)]
                qh = (q * jnp.exp2(b2 - bcast_rows(c, r_rows))).astype(BF16)
                kh = (k * jnp.exp2(bcast_rows(c, e_rows) - b2)).astype(BF16)
                per_level.append(_dot_nt(qh, block_diag(kh)))
            s8["levels"].append(per_level)
        for c in range(n_chunks):
            q = s8["q"][c]
            fprod = None
            for d in range(A_BAND):
                pd = q * _get(kz_ref, (c,), zl - d, A_CHUNK)
                if d >= 1:
                    fs = _get(fz_ref, (c,), zl - (d - 1), A_CHUNK)
                    fprod = fs if fprod is None else fprod * fs
                    pd = pd * fprod
                pall_ref[c, d * A_CHUNK:(d + 1) * A_CHUNK, :] = pd.astype(BF16)
            s8["band"].append(_dot(pall_ref[c], bd16))
        for c in range(n_chunks):
            khat = (s8["k"][c] * jnp.exp2(s8["b2_last"][c] - s8["b2"][c])).astype(BF16)
            s8["upd"].append(_dot_tn(s8["v"][c].astype(BF16), khat))

    def hgrn_outputs(s8):
        sts = [st_ref[...]]
        for c in range(n_chunks):
            sts.append(sts[c] * jnp.exp2(s8["b2_last"][c]) + s8["upd"][c] * bd32)
        st_ref[...] = sts[n_chunks]
        s8["o"] = []
        for c in range(n_chunks):
            q, v = s8["q"][c], s8["v"][c]
            scores = sum(s8["levels"][c][li] * lmask_ref[li] for li in range(n_lev))
            o = _dot_nt((q * jnp.exp2(s8["b2"][c])).astype(BF16), sts[c].astype(BF16))
            o = o + _dot(scores.astype(BF16), block_diag(v.astype(BF16)))
            for d in range(A_BAND):
                o = o + s8["band"][c][d * A_CHUNK:(d + 1) * A_CHUNK, :] * _get(vz_ref, (c,), zl - d, A_CHUNK)
            s8["o"].append(o)

    def hgrn_finish(s8):
        sq = []
        for c in range(n_chunks):
            o2 = s8["o"][c] * s8["o"][c]
            sq.append(_dot(o2.astype(BF16), bd16))
        for c in range(n_chunks):
            rows = chunk_rows(c)
            ms = sq[c] * (1.0 / A_HEAD_DIM)
            ya = s8["o"][c] * lax.rsqrt(ms + EPS) * a_norm_g * _silu(p_ref[rows, 3 * gw:4 * gw])
            ycat_ref[rows, 0:gw] = ya.astype(BF16)

    def conformer():
        _put(bbuf_ref, (), B_HALO, p_ref[:, o_b:o_b + gw] * _sigmoid(p_ref[:, o_b + gw:o_b + 2 * gw]))
        blk = MIX_BLOCK
        for rb in range(tt // blk):
            r0 = rb * blk
            base = B_HALO + r0 - (B_CONV_WIDTH - 1)
            acc = jnp.zeros((blk, gw), F32)
            for j in range(B_CONV_WIDTH):
                acc = acc + dww_ref[j:j + 1, :] * _get(bbuf_ref, (), base + j, blk)
            acc = acc + dwb_ref[...]
            mu = jnp.mean(acc, axis=-1, keepdims=True)
            cen = acc - mu
            var = jnp.mean(cen * cen, axis=-1, keepdims=True)
            yn = cen * lax.rsqrt(var + EPS) * lng_ref[...] + lnb_ref[...]
            yb = _dot(_silu(yn).astype(BF16), pww_ref[...]) + pwb_ref[...]
            ycat_ref[r0:r0 + blk, gw:2 * gw] = yb.astype(BF16)

    def short_conv():
        _put(cbuf_ref, (), C_HALO, p_ref[:, o_c + gw:o_c + 2 * gw] * p_ref[:, o_c + 2 * gw:o_c + 3 * gw])
        conv_c = jnp.zeros((tt, gw), F32)
        for j in range(C_CONV_WIDTH):
            off = C_HALO - (C_CONV_WIDTH - 1) + j
            conv_c = conv_c + ccw_ref[j:j + 1, :] * _get(cbuf_ref, (), off, tt)
        ycat_ref[:, 2 * gw:3 * gw] = (p_ref[:, o_c:o_c + gw] * conv_c).astype(BF16)

    def pooling():
        n = tt
        u = p_ref[:, o_d:o_d + gw]
        base = D_HALO
        _put(dbuf_ref, (), base, u)

        def window_sum(slab, w_small, w_big):
            s = dbuf_ref[slab, base:base + n, :]
            for j in range(1, w_small):
                s = s + dbuf_ref[slab, base - j:base - j + n, :]
            s_big = s
            for j in range(w_small, w_big):
                s_big = s_big + dbuf_ref[slab, base - j:base - j + n, :]
            lane = lax.broadcasted_iota(jnp.int32, (n, LANES), 1)
            return jnp.where(lane < LANES // 2, s, s_big)

        wsum = jnp.concatenate([window_sum(0, D_WINDOWS[0], D_WINDOWS[1]),
                                window_sum(1, D_WINDOWS[2], D_WINDOWS[3])], axis=1)
        pos1 = (ti * tt + 1 + lax.broadcasted_iota(jnp.int32, (n, gw), 0)).astype(F32)
        pooled = wsum / jnp.minimum(pos1, dwin_ref[...]) - u
        yd = _dot(pooled.astype(BF16), dproj_ref[...]) * dscale_ref[...]
        ycat_ref[:, 3 * gw:4 * gw] = yd.astype(BF16)

    h = _rms_norm(x_ref[0], preg_ref[...]).astype(BF16)
    for lo_c, hi_c in groups:
        p_ref[:, lo_c:hi_c] = _dot(h, w_in_ref[:, lo_c:hi_c])
    s8 = hgrn_prepare()
    hgrn_pair_dots(s8)
    hgrn_outputs(s8)
    hgrn_finish(s8)
    conformer()
    short_conv()
    pooling()
    y = _dot(ycat_ref[...], w_out_ref[...])
    o_ref[0] = x_ref[0] + _rms_norm(y, postg_ref[...])

    _put(bbuf_ref, (), 0, _get(bbuf_ref, (), tt, B_HALO))
    _put(cbuf_ref, (), 0, _get(cbuf_ref, (), tt, C_HALO))
    _put(dbuf_ref, (), 0, _get(dbuf_ref, (), tt, D_HALO))


def _const_spec(shape):
    nd = len(shape)
    return pl.BlockSpec(shape, lambda b, t, _nd=nd: (0,) * _nd)


def _layer_spec(shape, layer):
    nd = len(shape)
    return pl.BlockSpec((None,) + tuple(shape[1:]), lambda b, t, _nd=nd: (layer,) + (0,) * (_nd - 1))


def _mixer_call(layer, x, w_in, lb_gamma, a_norm_g, dw_w, dw_b, ln_g, ln_b, pw_w, pw_b, cc_w,
                dproj_bd, d_scale, w_out, pre_g, post_g, bd16, bd32, lmask, dwin):
    bsz, t_len, d = x.shape
    tt = MIX_TILE
    assert t_len % tt == 0 and tt % A_CHUNK == 0 and A_CHUNK % MIX_BLOCK == 0
    gw = GROUP_WIDTH
    consts = (w_in, lb_gamma, a_norm_g, dw_w, dw_b, ln_g, ln_b, pw_w, pw_b, cc_w, dproj_bd,
              d_scale, w_out, pre_g, post_g, bd16, bd32, lmask, dwin)
    shared = (lb_gamma, bd16, bd32, lmask, dwin)
    const_specs = [_const_spec(c.shape) if any(c is sh for sh in shared) else _layer_spec(c.shape, layer)
                   for c in consts]
    tile_spec = pl.BlockSpec((1, tt, d), lambda b, t: (b, t, 0))
    slab = lambda rows: pltpu.VMEM((GROUP_SLABS, rows, LANES), F32)
    n_chunks = tt // A_CHUNK
    chunk_slab = lambda rows: pltpu.VMEM((n_chunks, GROUP_SLABS, rows, LANES), F32)
    return pl.pallas_call(
        functools.partial(_mixer_kernel, layer),
        name=f"mixer_l{layer}",
        grid=(bsz, t_len // tt),
        in_specs=[tile_spec] + const_specs,
        out_specs=tile_spec,
        out_shape=jax.ShapeDtypeStruct(x.shape, x.dtype),
        scratch_shapes=[
            pltpu.VMEM((tt, IN_COLS), F32),
            pltpu.VMEM((tt, d), BF16),
            pltpu.VMEM((gw, gw), F32),
            chunk_slab(SCAN_HALO + A_CHUNK),
            chunk_slab(A_BAND + A_CHUNK),
            chunk_slab(A_BAND + A_CHUNK),
            chunk_slab(A_BAND + A_CHUNK),
            pltpu.VMEM((n_chunks, A_BAND * A_CHUNK, gw), BF16),
            slab(B_HALO + tt),
            slab(C_HALO + tt),
            slab(D_HALO + tt),
        ],
        compiler_params=pltpu.CompilerParams(
            dimension_semantics=("arbitrary", "arbitrary"),
            vmem_limit_bytes=VMEM_LIMIT_BYTES),
    )(x, *consts)


def _ffn_kernel(x_ref, w_up_ref, cw_ref, w_down_ref, preg_ref, postg_ref, o_ref,
                h_ref, ubuf_ref, act_ref):
    tt = x_ref.shape[1]
    ti = pl.program_id(1)
    half_slabs = FFN_CHUNK // LANES
    halves = (slice(0, tt // 2), slice(tt // 2, tt))

    @pl.when(ti == 0)
    def _reset():
        ubuf_ref[:, :, tt:tt + C_HALO, :] = jnp.zeros((N_FFN_CHUNKS, FFN_SLABS, C_HALO, LANES), F32)

    for rs in halves:
        h_ref[rs, :] = _rms_norm(x_ref[0, rs, :], preg_ref[...]).astype(BF16)

    def columns(c):
        return (c * FFN_CHUNK, D_FF + c * FFN_CHUNK)

    def up_project(c, row_slices):
        ubuf_ref[c, :, 0:C_HALO, :] = ubuf_ref[c, :, tt:tt + C_HALO, :]
        for rs in row_slices:
            for part, col0 in enumerate(columns(c)):
                u = _dot(h_ref[rs, :], w_up_ref[:, col0:col0 + FFN_CHUNK])
                _put(ubuf_ref, (c,), C_HALO + rs.start, u, slab0=part * half_slabs)

    def conv_act(c):
        for rs in halves:
            n = rs.stop - rs.start
            convs = []
            for part, col0 in enumerate(columns(c)):
                conv = jnp.zeros((n, FFN_CHUNK), F32)
                for j in range(FFN_CONV_WIDTH):
                    off = C_HALO - (FFN_CONV_WIDTH - 1) + j + rs.start
                    conv = conv + cw_ref[j:j + 1, col0:col0 + FFN_CHUNK] * _get(
                        ubuf_ref, (c,), off, n, slab0=part * half_slabs, slabs=half_slabs)
                convs.append(conv)
            act = _silu(convs[0]) * convs[1]
            act_ref[rs, c * FFN_CHUNK:(c + 1) * FFN_CHUNK] = act.astype(BF16)

    up_project(0, halves)
    for c in range(N_FFN_CHUNKS):
        if c + 1 < N_FFN_CHUNKS:
            up_project(c + 1, (slice(0, tt),))
        conv_act(c)

    for rs in halves:
        y = _dot(act_ref[rs, :], w_down_ref[...])
        o_ref[0, rs, :] = x_ref[0, rs, :] + _rms_norm(y, postg_ref[...])


def _ffn_call(layer, x, w_up, conv_w, w_down, pre_g, post_g):
    bsz, t_len, d = x.shape
    tt = FFN_TILE
    assert t_len % tt == 0
    consts = (w_up, conv_w, w_down, pre_g, post_g)
    tile_spec = pl.BlockSpec((1, tt, d), lambda b, t: (b, t, 0))
    return pl.pallas_call(
        _ffn_kernel,
        name="ffn",
        grid=(bsz, t_len // tt),
        in_specs=[tile_spec] + [_layer_spec(c.shape, layer) for c in consts],
        out_specs=tile_spec,
        out_shape=jax.ShapeDtypeStruct(x.shape, x.dtype),
        scratch_shapes=[
            pltpu.VMEM((tt, d), BF16),
            pltpu.VMEM((N_FFN_CHUNKS, FFN_SLABS, C_HALO + tt, LANES), F32),
            pltpu.VMEM((tt, D_FF), BF16),
        ],
        compiler_params=pltpu.CompilerParams(
            dimension_semantics=("arbitrary", "arbitrary"),
            vmem_limit_bytes=VMEM_LIMIT_BYTES),
    )(x, *consts)


def _mixer_constants():
    hs = np.arange(GROUP_WIDTH)
    bd = (hs[:, None] // A_HEAD_DIM == hs[None, :] // A_HEAD_DIM).astype(np.float32)
    t = np.arange(A_CHUNK)[:, None]
    s = (np.arange(GROUP_WIDTH) % A_CHUNK)[None, :]
    lmask = np.stack([((t // c) % 2 == 1) & (s // c == t // c - 1) & (t - s >= A_BAND)
                      for c in A_LEVELS]).astype(np.float32)
    dwin = np.repeat(np.asarray(D_WINDOWS, np.float32), GROUP_WIDTH // len(D_WINDOWS))[None, :]
    return jnp.asarray(bd, BF16), jnp.asarray(bd), jnp.asarray(lmask), jnp.asarray(dwin)


def _block_diag(proj):
    depth, g, n, _ = proj.shape
    out = jnp.zeros((depth, g * n, g * n), proj.dtype)
    for i in range(g):
        out = out.at[:, i * n:(i + 1) * n, i * n:(i + 1) * n].set(proj[:, i])
    return out


def kernel(x, w_in, lb_gamma, a_norm_g, b_dw_w, b_dw_b, b_ln_g, b_ln_b, b_pw_w, b_pw_b, c_conv_w, d_proj, d_scale, w_out, mix_pre_g, mix_post_g, ffn_pre_g, ffn_post_g, w_up, ffn_conv_w, w_down):
    depth = w_in.shape[0]
    bd16, bd32, lmask, dwin = _mixer_constants()
    rows = lambda a: a[:, None, :]
    mixer_params = (
        w_in.astype(BF16), lb_gamma, rows(a_norm_g), b_dw_w, rows(b_dw_b), rows(b_ln_g), rows(b_ln_b),
        b_pw_w.astype(BF16), rows(b_pw_b), c_conv_w, _block_diag(d_proj).astype(BF16), rows(d_scale),
        w_out.astype(BF16), rows(mix_pre_g), rows(mix_post_g), bd16, bd32, lmask, dwin)
    ffn_params = (w_up.astype(BF16), ffn_conv_w, w_down.astype(BF16), rows(ffn_pre_g), rows(ffn_post_g))
    for l in range(depth):
        x = _mixer_call(l, x, *mixer_params)
        x = _ffn_call(l, x, *ffn_params)
    return x
```
